```python
import jax
import jax.numpy as jnp
from jax import lax
import numpy as np

D_MODEL = 2048
BATCH = 4
SEQ = 8192
DEPTH = 4
DEC_BATCH = 8
DEC_SEQ = 2048
PAST_LEN = 128

D_MIX = D_MODEL
HEAD_DIM = 64
D_FOURIER = D_MIX // 4
FOURIER_GROUPS = 4
D_ATTN = (3 * D_MIX) // 8
N_HEADS = D_ATTN // HEAD_DIM
D_CONV = D_MIX - D_FOURIER - D_ATTN
CONV_WIDTH = 31
CONV_PAD = CONV_WIDTH // 2
DIL_PAIRS = ((128, 1), (512, 4), (2048, 16))
ATTN_BLOCK = 64
ROPE_THETA = 500000.0
ROPE_DIM = HEAD_DIM // 4
N_EXPERTS = 16
D_EXPERT = D_MODEL
CAPACITY_FACTOR = 2
DN_ALPHA = (2 * DEPTH) ** 0.25
DN_BETA = (8 * DEPTH) ** -0.25
LN_EPS = 1e-5
NEG_INF = -1e30
D_IN = 3 * D_ATTN + D_FOURIER + 2 * D_CONV
SPLITS = [D_ATTN, 2 * D_ATTN, 3 * D_ATTN, 3 * D_ATTN + D_FOURIER, 3 * D_ATTN + D_FOURIER + D_CONV]

kernel_name = 'hybrid_fourier_dilattn_conv_ecmoe_encoder'


def layer_norm(x, g, b):
    xf = x.astype(jnp.float32)
    mu = jnp.mean(xf, axis=-1, keepdims=True)
    xc = xf - mu
    var = jnp.mean(xc * xc, axis=-1, keepdims=True)
    y = xc * lax.rsqrt(var + LN_EPS) * g.astype(jnp.float32) + b.astype(jnp.float32)
    return y.astype(x.dtype)


def partial_rope(x, pos):
    half = ROPE_DIM // 2
    inv_freq = ROPE_THETA ** (-jnp.arange(0, ROPE_DIM, 2, dtype=jnp.float32) / ROPE_DIM)
    ang = pos[:, None] * inv_freq[None, :]
    cos = jnp.cos(ang)[None, :, None, :]
    sin = jnp.sin(ang)[None, :, None, :]
    xr = x[..., :ROPE_DIM].astype(jnp.float32)
    x1, x2 = xr[..., :half], xr[..., half:]
    rot = jnp.concatenate([x1 * cos - x2 * sin, x2 * cos + x1 * sin], axis=-1).astype(x.dtype)
    return jnp.concatenate([rot, x[..., ROPE_DIM:]], axis=-1)


def banded_attention(q, k, v, radius):
    n, l, h, hd = q.shape
    blk = ATTN_BLOCK
    nb = -(-l // blk)
    lp = nb * blk
    pad = lp - l
    qb = jnp.pad(q, ((0, 0), (0, pad), (0, 0), (0, 0))).reshape(n, nb, blk, h, hd)

    def neighbours(t):
        tp = jnp.pad(t, ((0, 0), (blk, pad + blk), (0, 0), (0, 0)))
        return jnp.concatenate(
            [tp[:, j * blk:j * blk + lp].reshape(n, nb, blk, h, hd) for j in range(3)], axis=2)

    kb = neighbours(k)
    vb = neighbours(v)
    qpos = jnp.arange(lp).reshape(nb, blk)
    kpos = (jnp.arange(nb)[:, None] - 1) * blk + jnp.arange(3 * blk)[None, :]
    dist = qpos[:, :, None] - kpos[:, None, :]
    valid = (jnp.abs(dist) <= radius) & (kpos[:, None, :] >= 0) & (kpos[:, None, :] < l)
    s = jnp.einsum('nbqhd,nbkhd->nbhqk', qb, kb, preferred_element_type=jnp.float32) * (hd ** -0.5)
    s = jnp.where(valid[None, :, None], s, NEG_INF)
    m = jnp.max(s, axis=-1, keepdims=True)
    p = jnp.exp(s - m)
    den = jnp.sum(p, axis=-1, keepdims=True)
    o = jnp.einsum('nbhqk,nbkhd->nbqhd', p, vb.astype(jnp.float32)) / jnp.transpose(den, (0, 1, 3, 2, 4))
    lse = (m + jnp.log(den))[..., 0]
    o = o.reshape(n, lp, h, hd)[:, :l]
    lse = jnp.transpose(lse, (0, 1, 3, 2)).reshape(n, lp, h)[:, :l]
    return o, lse


def dilated_branch(q, k, v, dil, radius):
    b, s, h, hd = q.shape
    l = s // dil

    def split(t):
        return t.reshape(b, l, dil, h, hd).transpose(0, 2, 1, 3, 4).reshape(b * dil, l, h, hd)

    o, lse = banded_attention(split(q), split(k), split(v), radius)
    o = o.reshape(b, dil, l, h, hd).transpose(0, 2, 1, 3, 4).reshape(b, s, h, hd)
    lse = lse.reshape(b, dil, l, h).transpose(0, 2, 1, 3).reshape(b, s, h)
    return o, lse


def dilated_attention(q, k, v):
    outs, lses = [], []
    for window, dil in DIL_PAIRS:
        o, lse = dilated_branch(q, k, v, dil, window // (2 * dil))
        outs.append(o)
        lses.append(lse)
    wts = jax.nn.softmax(jnp.stack(lses, axis=0), axis=0)
    y = wts[0][..., None] * outs[0]
    for i in range(1, len(DIL_PAIRS)):
        y = y + wts[i][..., None] * outs[i]
    return y


def fourier_mix(u):
    b, s, _ = u.shape
    uf = u.astype(jnp.float32).reshape(b, s, FOURIER_GROUPS, D_FOURIER // FOURIER_GROUPS)
    y = jnp.fft.fft2(uf, axes=(1, 3), norm='ortho').real
    return y.reshape(b, s, D_FOURIER).astype(u.dtype)


def conformer_conv(ua, ug, conv_w, conv_b, conv_ln_g, conv_ln_b):
    h = ua * jax.nn.sigmoid(ug)
    rhs = conv_w.reshape(CONV_WIDTH, 1, D_CONV).astype(h.dtype)
    h = lax.conv_general_dilated(h, rhs, (1,), [(CONV_PAD, CONV_PAD)],
                                 dimension_numbers=('NWC', 'WIO', 'NWC'),
                                 feature_group_count=D_CONV) + conv_b.astype(h.dtype)
    h = layer_norm(h, conv_ln_g, conv_ln_b)
    return jax.nn.silu(h)


def expert_choice_ffn(x, w_router, w_gate, w_up, w_down):
    b, s, d = x.shape
    n = b * s
    cap = max(1, CAPACITY_FACTOR * n // N_EXPERTS)
    xt = x.reshape(n, d)
    aff = jax.nn.softmax((xt @ w_router).astype(jnp.float32), axis=-1)
    gate, idx = lax.top_k(aff.T, cap)
    xe = xt[idx]
    hid = jax.nn.silu(jnp.einsum('ecd,edf->ecf', xe, w_gate)) * jnp.einsum('ecd,edf->ecf', xe, w_up)
    ye = jnp.einsum('ecf,efd->ecd', hid, w_down) * gate[..., None].astype(x.dtype)
    y = jnp.zeros_like(xt).at[idx.reshape(-1)].add(ye.reshape(-1, d))
    return y.reshape(b, s, d)


def encoder_layer(x, pos, w_in, conv_w, conv_b, conv_ln_g, conv_ln_b, w_out, ln1_g, ln1_b,
                  w_router, w_gate, w_up, w_down, ln2_g, ln2_b):
    b, s, _ = x.shape
    u = x @ w_in
    q, k, v, uf, ua, ug = jnp.split(u, SPLITS, axis=-1)
    q = partial_rope(q.reshape(b, s, N_HEADS, HEAD_DIM), pos)
    k = partial_rope(k.reshape(b, s, N_HEADS, HEAD_DIM), pos)
    v = v.reshape(b, s, N_HEADS, HEAD_DIM)
    y_attn = dilated_attention(q, k, v).astype(x.dtype).reshape(b, s, D_ATTN)
    y_four = fourier_mix(uf)
    y_conv = conformer_conv(ua, ug, conv_w, conv_b, conv_ln_g, conv_ln_b)
    mix = jnp.concatenate([y_four, y_attn, y_conv], axis=-1) @ w_out
    x = layer_norm(DN_ALPHA * x + mix, ln1_g, ln1_b)
    x = layer_norm(DN_ALPHA * x + expert_choice_ffn(x, w_router, w_gate, w_up, w_down), ln2_g, ln2_b)
    return x


def encoder_trunk(x, emb_ln_g, emb_ln_b, w_in, conv_w, conv_b, conv_ln_g, conv_ln_b, w_out,
                  ln1_g, ln1_b, w_router, w_gate, w_up, w_down, ln2_g, ln2_b):
    pos = jnp.arange(x.shape[1], dtype=jnp.float32)
    x = layer_norm(x, emb_ln_g, emb_ln_b)
    for l in range(DEPTH):
        x = encoder_layer(x, pos, w_in[l], conv_w[l], conv_b[l], conv_ln_g[l], conv_ln_b[l], w_out[l],
                          ln1_g[l], ln1_b[l], w_router[l], w_gate[l], w_up[l], w_down[l],
                          ln2_g[l], ln2_b[l])
    return x


def setup_inputs(seed: int = 0) -> dict:
    key = jax.random.key(seed)
    ks = jax.random.split(key, 20)
    nrm = jax.random.normal
    f32 = jnp.float32
    x_prompt = nrm(ks[0], (BATCH, SEQ, D_MODEL), f32)
    x_sample = nrm(ks[1], (DEC_BATCH, DEC_SEQ, D_MODEL), f32)
    emb_ln_g = 1.0 + 0.02 * nrm(ks[2], (D_MODEL,), f32)
    emb_ln_b = 0.02 * nrm(ks[3], (D_MODEL,), f32)
    col_scale = jnp.ones((D_IN,), f32).at[2 * D_ATTN:3 * D_ATTN].set(DN_BETA)
    w_in = nrm(ks[4], (DEPTH, D_MODEL, D_IN), f32) * (D_MODEL ** -0.5) * col_scale
    conv_w = nrm(ks[5], (DEPTH, CONV_WIDTH, D_CONV), f32) * (CONV_WIDTH ** -0.5)
    conv_b = 0.02 * nrm(ks[6], (DEPTH, D_CONV), f32)
    conv_ln_g = 1.0 + 0.02 * nrm(ks[7], (DEPTH, D_CONV), f32)
    conv_ln_b = 0.02 * nrm(ks[8], (DEPTH, D_CONV), f32)
    w_out = nrm(ks[9], (DEPTH, D_MIX, D_MODEL), f32) * (D_MIX ** -0.5) * DN_BETA
    ln1_g = 1.0 + 0.02 * nrm(ks[10], (DEPTH, D_MODEL), f32)
    ln1_b = 0.02 * nrm(ks[11], (DEPTH, D_MODEL), f32)
    w_router = nrm(ks[12], (DEPTH, D_MODEL, N_EXPERTS), f32) * (D_MODEL ** -0.5)
    w_gate = nrm(ks[13], (DEPTH, N_EXPERTS, D_MODEL, D_EXPERT), f32) * (D_MODEL ** -0.5)
    w_up = nrm(ks[14], (DEPTH, N_EXPERTS, D_MODEL, D_EXPERT), f32) * (D_MODEL ** -0.5)
    w_down = nrm(ks[15], (DEPTH, N_EXPERTS, D_EXPERT, D_MODEL), f32) * (D_EXPERT ** -0.5) * DN_BETA
    ln2_g = 1.0 + 0.02 * nrm(ks[16], (DEPTH, D_MODEL), f32)
    ln2_b = 0.02 * nrm(ks[17], (DEPTH, D_MODEL), f32)
    return {'x_prompt': x_prompt, 'x_sample': x_sample, 'emb_ln_g': emb_ln_g, 'emb_ln_b': emb_ln_b,
            'w_in': w_in, 'conv_w': conv_w, 'conv_b': conv_b, 'conv_ln_g': conv_ln_g,
            'conv_ln_b': conv_ln_b, 'w_out': w_out, 'ln1_g': ln1_g, 'ln1_b': ln1_b,
            'w_router': w_router, 'w_gate': w_gate, 'w_up': w_up, 'w_down': w_down,
            'ln2_g': ln2_g, 'ln2_b': ln2_b}


def reference(x_prompt, x_sample, emb_ln_g, emb_ln_b, w_in, conv_w, conv_b, conv_ln_g, conv_ln_b,
              w_out, ln1_g, ln1_b, w_router, w_gate, w_up, w_down, ln2_g, ln2_b):
    y_prompt = encoder_trunk(x_prompt, emb_ln_g, emb_ln_b, w_in, conv_w, conv_b, conv_ln_g, conv_ln_b,
                             w_out, ln1_g, ln1_b, w_router, w_gate, w_up, w_down, ln2_g, ln2_b)
    y_sample = encoder_trunk(x_sample, emb_ln_g, emb_ln_b, w_in, conv_w, conv_b, conv_ln_g, conv_ln_b,
                             w_out, ln1_g, ln1_b, w_router, w_gate, w_up, w_down, ln2_g, ln2_b)
    return (y_prompt, y_sample)
```

```python
import functools
import math

import jax
import jax.numpy as jnp
from jax import lax
from jax.experimental import pallas as pl
from jax.experimental.pallas import tpu as pltpu

F32 = jnp.float32
MXU_DTYPE = jnp.bfloat16

HEAD_DIM = 64
FOURIER_GROUPS = 4
CONV_WIDTH = 31
CONV_PAD = CONV_WIDTH // 2
DILATIONS = (1, 4, 16)
ATTN_RADIUS = 64
ATTN_BLOCK = 64
ROPE_THETA = 500000.0
ROPE_DIM = HEAD_DIM // 4
N_EXPERTS = 16
CAPACITY_FACTOR = 2
LN_EPS = 1e-5
NEG_INF = -1e30

LANES = 128
SUBLANES = 8
VMEM_LIMIT = 56 * 1024 * 1024

ROW_TILE = 512
ATTN_SUPER = ATTN_BLOCK * max(DILATIONS)
KEY_WINDOW = 3 * ATTN_BLOCK
CONV_HALO = 16
CONV_ROWS = 64
FFN_ROW_TILE = 1024
FFN_HID_TILE = 512
DFT_TILE = 1024


def _params(*sem):
    return pltpu.CompilerParams(dimension_semantics=sem, vmem_limit_bytes=VMEM_LIMIT)


def _layer_norm(x, g, b):
    mu = jnp.mean(x, axis=-1, keepdims=True)
    xc = x - mu
    var = jnp.mean(xc * xc, axis=-1, keepdims=True)
    return xc * lax.rsqrt(var + LN_EPS) * g + b


def _emb_ln_kernel(x_ref, g_ref, b_ref, xf_ref, xb_ref):
    y = _layer_norm(x_ref[...], g_ref[...], b_ref[...])
    xf_ref[...] = y
    xb_ref[...] = y.astype(xb_ref.dtype)


def _emb_ln(x, g, b):
    n, d = x.shape
    row = pl.BlockSpec((ROW_TILE, d), lambda i: (i, 0))
    vec = pl.BlockSpec((1, d), lambda i: (0, 0))
    return pl.pallas_call(
        _emb_ln_kernel,
        grid=(n // ROW_TILE,),
        in_specs=[row, vec, vec],
        out_specs=[row, row],
        out_shape=[jax.ShapeDtypeStruct((n, d), F32), jax.ShapeDtypeStruct((n, d), MXU_DTYPE)],
        compiler_params=_params("parallel"),
        name="emb_ln",
    )(x, g.reshape(1, d), b.reshape(1, d))


def _in_proj_kernel(x_ref, w_ref, cos_ref, sa_ref, sb_ref, qkv_ref, uf_ref, uc_ref, *, d_attn, d_four):
    x = x_ref[...]
    cos, sa, sb = cos_ref[...], sa_ref[...], sb_ref[...]
    half = ROPE_DIM // 2
    chunk = 2 * LANES
    d_qkv = 3 * d_attn
    for c in range(0, d_qkv, chunk):
        acc = jnp.dot(x, w_ref[:, c:c + chunk], preferred_element_type=F32)
        if c < 2 * d_attn:
            for h in range(0, chunk, LANES):
                a = acc[:, h:h + LANES]
                rot = a * cos + pltpu.roll(a, LANES - half, 1) * sa + pltpu.roll(a, half, 1) * sb
                qkv_ref[:, c + h:c + h + LANES] = rot
        else:
            qkv_ref[:, c:c + chunk] = acc
    for c in range(0, d_four, chunk):
        acc = jnp.dot(x, w_ref[:, d_qkv + c:d_qkv + c + chunk], preferred_element_type=F32)
        uf_ref[:, c:c + chunk] = acc.astype(uf_ref.dtype)
    d_conv2 = uc_ref.shape[1]
    for c in range(0, d_conv2, chunk):
        off = d_qkv + d_four + c
        uc_ref[:, c:c + chunk] = jnp.dot(x, w_ref[:, off:off + chunk], preferred_element_type=F32)


def _rope_tables(seq):
    half = ROPE_DIM // 2
    pos = jnp.arange(seq, dtype=F32)
    inv_freq = ROPE_THETA ** (-jnp.arange(0, ROPE_DIM, 2, dtype=F32) / ROPE_DIM)
    ang = pos[:, None] * inv_freq[None, :]
    cos, sin = jnp.cos(ang), jnp.sin(ang)
    rest = HEAD_DIM - ROPE_DIM
    ones = jnp.ones((seq, rest), F32)
    zeros = jnp.zeros((seq, rest), F32)
    zh = jnp.zeros((seq, half), F32)
    cos_h = jnp.concatenate([cos, cos, ones], axis=1)
    sa_h = jnp.concatenate([-sin, zh, zeros], axis=1)
    sb_h = jnp.concatenate([zh, sin, zeros], axis=1)
    reps = LANES // HEAD_DIM
    return tuple(jnp.tile(t, (1, reps)) for t in (cos_h, sa_h, sb_h))


def _in_proj(xb, w_in, rope, seq, d_attn, d_four):
    n, d = xb.shape
    d_in = w_in.shape[1]
    d_conv2 = d_in - 3 * d_attn - d_four
    tiles_per_seq = seq // ROW_TILE
    row = lambda w: pl.BlockSpec((ROW_TILE, w), lambda i: (i, 0))
    tab = pl.BlockSpec((ROW_TILE, LANES), lambda i: (i % tiles_per_seq, 0))
    return pl.pallas_call(
        functools.partial(_in_proj_kernel, d_attn=d_attn, d_four=d_four),
        grid=(n // ROW_TILE,),
        in_specs=[row(d), pl.BlockSpec((d, d_in), lambda i: (0, 0), pipeline_mode=pl.Buffered(1)), tab, tab, tab],
        out_specs=[row(3 * d_attn), row(d_four), row(d_conv2)],
        out_shape=[jax.ShapeDtypeStruct((n, 3 * d_attn), F32),
                   jax.ShapeDtypeStruct((n, d_four), MXU_DTYPE),
                   jax.ShapeDtypeStruct((n, d_conv2), F32)],
        compiler_params=_params("parallel"),
        name="in_proj",
    )(xb, w_in, *rope)


def _attn_kernel(q_ref, k_ref, v_ref, o_ref, osc, lsc, *, seq):
    lane = lax.broadcasted_iota(jnp.int32, (ATTN_BLOCK, LANES), 1)
    head_a = lane < HEAD_DIM
    scale = HEAD_DIM ** -0.5
    blocks_per_super = ATTN_SUPER // ATTN_BLOCK

    def strided(start, size, d):
        return pl.ds(start, size) if d == 1 else pl.ds(start, size, stride=d)

    def super_block(sb, carry):
        for bi, d in enumerate(DILATIONS):
            sub_len = seq // d
            kw = min(KEY_WINDOW, sub_len)
            per_res = blocks_per_super // d
            row = lax.broadcasted_iota(jnp.int32, (2 * ATTN_BLOCK, kw), 0) & (ATTN_BLOCK - 1)
            col = lax.broadcasted_iota(jnp.int32, (2 * ATTN_BLOCK, kw), 1)
            row_minus_col = row - col

            def block(j, c, d=d, bi=bi, sub_len=sub_len, kw=kw, per_res=per_res, row_minus_col=row_minus_col):
                r = j % d
                bl = j // d
                i = sb * per_res + bl
                ks = jnp.clip(ATTN_BLOCK * (i - 1), 0, sub_len - kw)
                q = q_ref[strided(r + d * ATTN_BLOCK * i, ATTN_BLOCK, d), :] * scale
                kk = k_ref[strided(r + d * ks, kw, d), :]
                vv = v_ref[strided(r + d * ks, kw, d), :]
                zero = jnp.zeros_like(q)
                q2 = jnp.concatenate([jnp.where(head_a, q, zero), jnp.where(head_a, zero, q)], axis=0)
                s = lax.dot_general(q2.astype(MXU_DTYPE), kk.astype(MXU_DTYPE), (((1,), (1,)), ((), ())),
                                    preferred_element_type=F32)
                dist = row_minus_col + (ATTN_BLOCK * i - ks)
                s = jnp.where(jnp.abs(dist) <= ATTN_RADIUS, s, NEG_INF)
                m = jnp.max(s, axis=-1, keepdims=True)
                p = jnp.exp(s - m)
                den = jnp.sum(p, axis=-1, keepdims=True)
                o2 = jnp.dot(p.astype(MXU_DTYPE), vv.astype(MXU_DTYPE), preferred_element_type=F32) / den
                lse2 = m + jnp.log(den)
                o = jnp.where(head_a, o2[:ATTN_BLOCK], o2[ATTN_BLOCK:])
                lse = jnp.where(head_a, lse2[:ATTN_BLOCK], lse2[ATTN_BLOCK:])
                dst = strided(r + d * ATTN_BLOCK * bl, ATTN_BLOCK, d)
                osc[bi, dst, :] = o
                lsc[bi, dst, :] = lse
                return c

            lax.fori_loop(0, blocks_per_super, block, 0)
        l0, l1, l2 = lsc[0], lsc[1], lsc[2]
        mx = jnp.maximum(jnp.maximum(l0, l1), l2)
        e0, e1, e2 = jnp.exp(l0 - mx), jnp.exp(l1 - mx), jnp.exp(l2 - mx)
        tot = e0 + e1 + e2
        y = (e0 / tot) * osc[0] + (e1 / tot) * osc[1] + (e2 / tot) * osc[2]
        o_ref[pl.ds(pl.multiple_of(sb * ATTN_SUPER, ATTN_SUPER), ATTN_SUPER), :] = y.astype(o_ref.dtype)
        return carry

    lax.fori_loop(0, seq // ATTN_SUPER, super_block, 0)


def _attention(qkv, batch, seq, d_attn):
    pairs = d_attn // LANES
    qkv3 = qkv.reshape(batch, seq, 3 * d_attn)
    spec = lambda off: pl.BlockSpec((None, seq, LANES), lambda b, h: (b, 0, off + h))
    out = pl.pallas_call(
        functools.partial(_attn_kernel, seq=seq),
        grid=(batch, pairs),
        in_specs=[spec(0), spec(pairs), spec(2 * pairs)],
        out_specs=pl.BlockSpec((None, seq, LANES), lambda b, h: (b, 0, h)),
        out_shape=jax.ShapeDtypeStruct((batch, seq, d_attn), MXU_DTYPE),
        scratch_shapes=[pltpu.VMEM((len(DILATIONS), ATTN_SUPER, LANES), F32),
                        pltpu.VMEM((len(DILATIONS), ATTN_SUPER, LANES), F32)],
        compiler_params=_params("parallel", "parallel"),
        name="dilated_attention",
    )(qkv3, qkv3, qkv3)
    return out.reshape(batch * seq, d_attn)


def _fourier_kernel(cs_ref, ss_ref, u_ref, cc_ref, o_ref, acc_ref, *, d_four):
    k = pl.program_id(2)

    @pl.when(k == 0)
    def _():
        acc_ref[...] = jnp.zeros_like(acc_ref)

    u = u_ref[...]
    acc_ref[:, :d_four] += jnp.dot(cs_ref[...], u, preferred_element_type=F32)
    acc_ref[:, d_four:] += jnp.dot(ss_ref[...], u, preferred_element_type=F32)

    @pl.when(k == pl.num_programs(2) - 1)
    def _():
        y = jnp.dot(acc_ref[...].astype(MXU_DTYPE), cc_ref[...], preferred_element_type=F32)
        o_ref[...] = y.astype(o_ref.dtype)


def _dft_matrices(seq, d_four):
    k = jnp.arange(seq, dtype=jnp.int32)
    ang = ((k[:, None] * k[None, :]) % seq).astype(F32) * (2.0 * math.pi / seq)
    cs = (jnp.cos(ang) * seq ** -0.5).astype(MXU_DTYPE)
    ss = (jnp.sin(ang) * seq ** -0.5).astype(MXU_DTYPE)
    gw = d_four // FOURIER_GROUPS
    c = jnp.arange(gw, dtype=jnp.int32)
    angc = ((c[:, None] * c[None, :]) % gw).astype(F32) * (2.0 * math.pi / gw)
    eye = jnp.eye(FOURIER_GROUPS, dtype=F32)
    cc = jnp.kron(eye, jnp.cos(angc) * gw ** -0.5)
    sc = jnp.kron(eye, jnp.sin(angc) * gw ** -0.5)
    chan = jnp.concatenate([cc, -sc], axis=0).astype(MXU_DTYPE)
    return cs, ss, chan


def _fourier(uf, dft, batch, seq):
    cs, ss, chan = dft
    d_four = uf.shape[1]
    t = min(DFT_TILE, seq)
    u3 = uf.reshape(batch, seq, d_four)
    mat = pl.BlockSpec((t, t), lambda b, i, k: (i, k))
    out = pl.pallas_call(
        functools.partial(_fourier_kernel, d_four=d_four),
        grid=(batch, seq // t, seq // t),
        in_specs=[mat, mat,
                  pl.BlockSpec((None, t, d_four), lambda b, i, k: (b, k, 0)),
                  pl.BlockSpec((2 * d_four, d_four), lambda b, i, k: (0, 0))],
        out_specs=pl.BlockSpec((None, t, d_four), lambda b, i, k: (b, i, 0)),
        out_shape=jax.ShapeDtypeStruct((batch, seq, d_four), MXU_DTYPE),
        scratch_shapes=[pltpu.VMEM((t, 2 * d_four), F32)],
        compiler_params=_params("parallel", "parallel", "arbitrary"),
        name="fourier_mix",
    )(cs, ss, u3, chan)
    return out.reshape(batch * seq, d_four)


def _conv_kernel(cur_ref, prev_ref, next_ref, w_ref, b_ref, g_ref, beta_ref, o_ref, hp_ref, acc_ref, *, d_conv):
    i = pl.program_id(1)
    last = pl.num_programs(1) - 1
    tile = cur_ref.shape[0]

    def glu(ref):
        return ref[:, :d_conv] * jax.nn.sigmoid(ref[:, d_conv:])

    hp_ref[pl.ds(0, CONV_HALO), :] = jnp.where(i > 0, glu(prev_ref), 0.0)
    hp_ref[pl.ds(CONV_HALO, tile), :] = glu(cur_ref)
    hp_ref[pl.ds(CONV_HALO + tile, CONV_HALO), :] = jnp.where(i < last, glu(next_ref), 0.0)

    first_tap = CONV_HALO - CONV_PAD
    for c in range(0, d_conv, LANES):
        taps = [w_ref[pl.ds(j, 1), c:c + LANES] for j in range(CONV_WIDTH)]

        def rows(rc, carry, c=c, taps=taps):
            base = pl.multiple_of(rc * CONV_ROWS, CONV_ROWS)
            win = hp_ref[pl.ds(base, CONV_ROWS + 2 * CONV_HALO), c:c + LANES]
            acc = jnp.zeros((CONV_ROWS, LANES), F32)
            for j in range(CONV_WIDTH):
                acc = acc + win[first_tap + j:first_tap + j + CONV_ROWS] * taps[j]
            acc_ref[pl.ds(base, CONV_ROWS), c:c + LANES] = acc
            return carry

        lax.fori_loop(0, tile // CONV_ROWS, rows, 0)

    h = _layer_norm(acc_ref[...] + b_ref[...], g_ref[...], beta_ref[...])
    o_ref[...] = (h * jax.nn.sigmoid(h)).astype(o_ref.dtype)


def _conv_module(uc, conv_w, conv_b, ln_g, ln_b, batch, seq):
    d_conv = uc.shape[1] // 2
    tile = ROW_TILE
    halo_per_tile = tile // CONV_HALO
    n_halo = seq // CONV_HALO
    u3 = uc.reshape(batch, seq, 2 * d_conv)
    vec = pl.BlockSpec((1, d_conv), lambda b, i: (0, 0))
    out = pl.pallas_call(
        functools.partial(_conv_kernel, d_conv=d_conv),
        grid=(batch, seq // tile),
        in_specs=[pl.BlockSpec((None, tile, 2 * d_conv), lambda b, i: (b, i, 0)),
                  pl.BlockSpec((None, CONV_HALO, 2 * d_conv),
                               lambda b, i: (b, jnp.maximum(i * halo_per_tile - 1, 0), 0)),
                  pl.BlockSpec((None, CONV_HALO, 2 * d_conv),
                               lambda b, i: (b, jnp.minimum((i + 1) * halo_per_tile, n_halo - 1), 0)),
                  pl.BlockSpec((CONV_WIDTH, d_conv), lambda b, i: (0, 0)),
                  vec, vec, vec],
        out_specs=pl.BlockSpec((None, tile, d_conv), lambda b, i: (b, i, 0)),
        out_shape=jax.ShapeDtypeStruct((batch, seq, d_conv), MXU_DTYPE),
        scratch_shapes=[pltpu.VMEM((tile + 2 * CONV_HALO, d_conv), F32), pltpu.VMEM((tile, d_conv), F32)],
        compiler_params=_params("parallel", "parallel"),
        name="conv_module",
    )(u3, u3, u3, conv_w, conv_b.reshape(1, d_conv), ln_g.reshape(1, d_conv), ln_b.reshape(1, d_conv))
    return out.reshape(batch * seq, d_conv)


def _out_proj_kernel(yf_ref, ya_ref, yc_ref, x_ref, w_ref, g_ref, b_ref, wr_ref, xf_ref, xb_ref, aff_ref, *, alpha):
    d_four, d_attn = yf_ref.shape[1], ya_ref.shape[1]
    mix = jnp.dot(yf_ref[...], w_ref[:d_four, :], preferred_element_type=F32)
    mix += jnp.dot(ya_ref[...], w_ref[d_four:d_four + d_attn, :], preferred_element_type=F32)
    mix += jnp.dot(yc_ref[...], w_ref[d_four + d_attn:, :], preferred_element_type=F32)
    x1 = _layer_norm(alpha * x_ref[...] + mix, g_ref[...], b_ref[...])
    xf_ref[...] = x1
    xb = x1.astype(xb_ref.dtype)
    xb_ref[...] = xb
    logits = jnp.dot(xb, wr_ref[...], preferred_element_type=F32)
    lt = logits.T[:N_EXPERTS, :]
    e = jnp.exp(lt - jnp.max(lt, axis=0, keepdims=True))
    aff_ref[...] = e / jnp.sum(e, axis=0, keepdims=True)


def _out_proj(yf, ya, yc, xf, w_out, g, b, w_router_pad, alpha):
    n, d = xf.shape
    row = lambda w: pl.BlockSpec((ROW_TILE, w), lambda i: (i, 0))
    vec = pl.BlockSpec((1, d), lambda i: (0, 0))
    return pl.pallas_call(
        functools.partial(_out_proj_kernel, alpha=alpha),
        grid=(n // ROW_TILE,),
        in_specs=[row(yf.shape[1]), row(ya.shape[1]), row(yc.shape[1]), row(d),
                  pl.BlockSpec(w_out.shape, lambda i: (0, 0), pipeline_mode=pl.Buffered(1)),
                  vec, vec,
                  pl.BlockSpec(w_router_pad.shape, lambda i: (0, 0), pipeline_mode=pl.Buffered(1))],
        out_specs=[row(d), row(d), pl.BlockSpec((N_EXPERTS, ROW_TILE), lambda i: (0, i))],
        out_shape=[jax.ShapeDtypeStruct((n, d), F32), jax.ShapeDtypeStruct((n, d), MXU_DTYPE),
                   jax.ShapeDtypeStruct((N_EXPERTS, n), F32)],
        compiler_params=_params("parallel"),
        name="out_proj_ln_router",
    )(yf, ya, yc, xf, w_out, g.reshape(1, d), b.reshape(1, d), w_router_pad)


def _ffn_kernel(x_ref, wg_ref, wu_ref, wd_ref, gate_ref, o_ref, acc_ref):
    f = pl.program_id(2)
    x = x_ref[...]
    g = jnp.dot(x, wg_ref[...], preferred_element_type=F32)
    u = jnp.dot(x, wu_ref[...], preferred_element_type=F32)
    hid = (g * jax.nn.sigmoid(g) * u).astype(MXU_DTYPE)
    part = jnp.dot(hid, wd_ref[...], preferred_element_type=F32)

    @pl.when(f == 0)
    def _():
        acc_ref[...] = part

    @pl.when(f > 0)
    def _():
        acc_ref[...] += part

    @pl.when(f == pl.num_programs(2) - 1)
    def _():
        o_ref[...] = acc_ref[...] * gate_ref[...]


def _expert_ffn(xe, w_gate, w_up, w_down, gate, cap):
    rows, d = xe.shape
    n_exp, _, d_hid = w_gate.shape
    tm = min(FFN_ROW_TILE, cap)
    tf = FFN_HID_TILE
    tiles = cap // tm
    return pl.pallas_call(
        _ffn_kernel,
        grid=(n_exp, tiles, d_hid // tf),
        in_specs=[pl.BlockSpec((tm, d), lambda e, i, f: (e * tiles + i, 0)),
                  pl.BlockSpec((None, d, tf), lambda e, i, f: (e, 0, f)),
                  pl.BlockSpec((None, d, tf), lambda e, i, f: (e, 0, f)),
                  pl.BlockSpec((None, tf, d), lambda e, i, f: (e, f, 0)),
                  pl.BlockSpec((tm, 1), lambda e, i, f: (e * tiles + i, 0))],
        out_specs=pl.BlockSpec((tm, d), lambda e, i, f: (e * tiles + i, 0)),
        out_shape=jax.ShapeDtypeStruct((rows, d), F32),
        scratch_shapes=[pltpu.VMEM((tm, d), F32)],
        compiler_params=_params("parallel", "parallel", "arbitrary"),
        name="expert_ffn",
    )(xe, w_gate, w_up, w_down, gate)


def _res_ln_kernel(x_ref, y_ref, g_ref, b_ref, xf_ref, xb_ref, *, alpha):
    x2 = _layer_norm(alpha * x_ref[...] + y_ref[...], g_ref[...], b_ref[...])
    xf_ref[...] = x2
    xb_ref[...] = x2.astype(xb_ref.dtype)


def _res_ln(xf, y, g, b, alpha):
    n, d = xf.shape
    row = pl.BlockSpec((ROW_TILE, d), lambda i: (i, 0))
    vec = pl.BlockSpec((1, d), lambda i: (0, 0))
    return pl.pallas_call(
        functools.partial(_res_ln_kernel, alpha=alpha),
        grid=(n // ROW_TILE,),
        in_specs=[row, row, vec, vec],
        out_specs=[row, row],
        out_shape=[jax.ShapeDtypeStruct((n, d), F32), jax.ShapeDtypeStruct((n, d), MXU_DTYPE)],
        compiler_params=_params("parallel"),
        name="residual_ln",
    )(xf, y, g.reshape(1, d), b.reshape(1, d))


def _moe(xf, xb, aff_t, w_gate, w_up, w_down):
    n, d = xf.shape
    cap = max(1, CAPACITY_FACTOR * n // N_EXPERTS)
    gate, idx = lax.top_k(aff_t, cap)
    flat = idx.reshape(-1)
    xe = jnp.take(xb, flat, axis=0)
    ye = _expert_ffn(xe, w_gate, w_up, w_down, gate.reshape(-1, 1), cap)
    return jnp.zeros((n, d), F32).at[flat].add(ye)


def _trunk(x, weights, depth, alpha):
    batch, seq, d = x.shape
    n = batch * seq
    d_attn = (3 * d) // 8
    d_four = d // 4
    rope = _rope_tables(seq)
    dft = _dft_matrices(seq, d_four)
    xf, xb = _emb_ln(x.reshape(n, d), weights["emb_ln_g"], weights["emb_ln_b"])
    for l in range(depth):
        w = {k: v[l] for k, v in weights.items() if not k.startswith("emb_")}
        qkv, uf, uc = _in_proj(xb, w["w_in"], rope, seq, d_attn, d_four)
        ya = _attention(qkv, batch, seq, d_attn)
        yf = _fourier(uf, dft, batch, seq)
        yc = _conv_module(uc, w["conv_w"], w["conv_b"], w["conv_ln_g"], w["conv_ln_b"], batch, seq)
        x1f, x1b, aff_t = _out_proj(yf, ya, yc, xf, w["w_out"], w["ln1_g"], w["ln1_b"], w["w_router"], alpha)
        y = _moe(x1f, x1b, aff_t, w["w_gate"], w["w_up"], w["w_down"])
        xf, xb = _res_ln(x1f, y, w["ln2_g"], w["ln2_b"], alpha)
    return xf.reshape(batch, seq, d)


def kernel(x_prompt, x_sample, emb_ln_g, emb_ln_b, w_in, conv_w, conv_b, conv_ln_g, conv_ln_b, w_out, ln1_g, ln1_b, w_router, w_gate, w_up, w_down, ln2_g, ln2_b):
    depth = w_in.shape[0]
    alpha = (2 * depth) ** 0.25
    w_router_pad = jnp.pad(w_router, ((0, 0), (0, 0), (0, LANES - w_router.shape[-1])))
    weights = dict(emb_ln_g=emb_ln_g, emb_ln_b=emb_ln_b, conv_w=conv_w, conv_b=conv_b, conv_ln_g=conv_ln_g,
                   conv_ln_b=conv_ln_b, ln1_g=ln1_g, ln1_b=ln1_b, ln2_g=ln2_g, ln2_b=ln2_b,
                   w_in=w_in.astype(MXU_DTYPE), w_out=w_out.astype(MXU_DTYPE),
                   w_router=w_router_pad.astype(MXU_DTYPE), w_gate=w_gate.astype(MXU_DTYPE),
                   w_up=w_up.astype(MXU_DTYPE), w_down=w_down.astype(MXU_DTYPE))
    y_prompt = _trunk(x_prompt, weights, depth, alpha)
    y_sample = _trunk(x_sample, weights, depth, alpha)
    return (y_prompt, y_sample)
```

```python
import functools
import math

import jax
import jax.numpy as jnp
from jax import lax
from jax.experimental import pallas as pl
from jax.experimental.pallas import tpu as pltpu

F32 = jnp.float32
MXU_DTYPE = jnp.bfloat16

HEAD_DIM = 64
FOURIER_GROUPS = 4
CONV_WIDTH = 31
CONV_PAD = CONV_WIDTH // 2
DILATIONS = (1, 4, 16)
ATTN_RADIUS = 64
ATTN_BLOCK = 64
ROPE_THETA = 500000.0
ROPE_DIM = HEAD_DIM // 4
N_EXPERTS = 16
CAPACITY_FACTOR = 2
LN_EPS = 1e-5
NEG_INF = -1e30

LANES = 128
SUBLANES = 8
VMEM_LIMIT = 56 * 1024 * 1024

ROW_TILE = 512
ATTN_SUPER = ATTN_BLOCK * max(DILATIONS)
KEY_WINDOW = 3 * ATTN_BLOCK
KEY_PAD = 2 * LANES
SOFTMAX_ROWS = 32
MIX_ROWS = 64
CONV_HALO = 16
CONV_ROWS = 64
FFN_ROW_TILE = 1024
FFN_HID_TILE = 512
SEG_TOKENS = 256
SEG_CHUNK = 256
SEG_ALIGN = 128
SEG_COLS = 512
DFT_TILE = 1024


def _params(*sem):
    return pltpu.CompilerParams(dimension_semantics=sem, vmem_limit_bytes=VMEM_LIMIT)


def _layer_norm(x, g, b):
    mu = jnp.mean(x, axis=-1, keepdims=True)
    xc = x - mu
    var = jnp.mean(xc * xc, axis=-1, keepdims=True)
    return xc * lax.rsqrt(var + LN_EPS) * g + b


def _emb_ln_kernel(x_ref, g_ref, b_ref, xf_ref, xb_ref):
    y = _layer_norm(x_ref[...], g_ref[...], b_ref[...])
    xf_ref[...] = y
    xb_ref[...] = y.astype(xb_ref.dtype)


def _emb_ln(x, g, b):
    n, d = x.shape
    row = pl.BlockSpec((ROW_TILE, d), lambda i: (i, 0))
    vec = pl.BlockSpec((1, d), lambda i: (0, 0))
    return pl.pallas_call(
        _emb_ln_kernel,
        grid=(n // ROW_TILE,),
        in_specs=[row, vec, vec],
        out_specs=[row, row],
        out_shape=[jax.ShapeDtypeStruct((n, d), F32), jax.ShapeDtypeStruct((n, d), MXU_DTYPE)],
        compiler_params=_params("parallel"),
        name="emb_ln",
    )(x, g.reshape(1, d), b.reshape(1, d))


def _in_proj_kernel(x_ref, w_ref, cos_ref, sa_ref, sb_ref, qkv_ref, uf_ref, uc_ref, *, d_attn, d_four):
    x = x_ref[...]
    cos, sa, sb = cos_ref[...], sa_ref[...], sb_ref[...]
    half = ROPE_DIM // 2
    chunk = 2 * LANES
    d_qkv = 3 * d_attn
    for c in range(0, d_qkv, chunk):
        acc = jnp.dot(x, w_ref[:, c:c + chunk], preferred_element_type=F32)
        if c < 2 * d_attn:
            for h in range(0, chunk, LANES):
                a = acc[:, h:h + LANES]
                rot = a * cos + pltpu.roll(a, LANES - half, 1) * sa + pltpu.roll(a, half, 1) * sb
                qkv_ref[:, c + h:c + h + LANES] = rot
        else:
            qkv_ref[:, c:c + chunk] = acc
    for c in range(0, d_four, chunk):
        acc = jnp.dot(x, w_ref[:, d_qkv + c:d_qkv + c + chunk], preferred_element_type=F32)
        uf_ref[:, c:c + chunk] = acc.astype(uf_ref.dtype)
    d_conv2 = uc_ref.shape[1]
    for c in range(0, d_conv2, chunk):
        off = d_qkv + d_four + c
        uc_ref[:, c:c + chunk] = jnp.dot(x, w_ref[:, off:off + chunk], preferred_element_type=F32)


def _rope_tables(seq):
    half = ROPE_DIM // 2
    pos = jnp.arange(seq, dtype=F32)
    inv_freq = ROPE_THETA ** (-jnp.arange(0, ROPE_DIM, 2, dtype=F32) / ROPE_DIM)
    ang = pos[:, None] * inv_freq[None, :]
    cos, sin = jnp.cos(ang), jnp.sin(ang)
    rest = HEAD_DIM - ROPE_DIM
    ones = jnp.ones((seq, rest), F32)
    zeros = jnp.zeros((seq, rest), F32)
    zh = jnp.zeros((seq, half), F32)
    cos_h = jnp.concatenate([cos, cos, ones], axis=1)
    sa_h = jnp.concatenate([-sin, zh, zeros], axis=1)
    sb_h = jnp.concatenate([zh, sin, zeros], axis=1)
    reps = LANES // HEAD_DIM
    return tuple(jnp.tile(t, (1, reps)) for t in (cos_h, sa_h, sb_h))


def _in_proj(xb, w_in, layer, rope, seq, d_attn, d_four):
    n, d = xb.shape
    d_in = w_in.shape[2]
    d_conv2 = d_in - 3 * d_attn - d_four
    tiles_per_seq = seq // ROW_TILE
    row = lambda w: pl.BlockSpec((ROW_TILE, w), lambda i: (i, 0))
    tab = pl.BlockSpec((ROW_TILE, LANES), lambda i: (i % tiles_per_seq, 0))
    return pl.pallas_call(
        functools.partial(_in_proj_kernel, d_attn=d_attn, d_four=d_four),
        grid=(n // ROW_TILE,),
        in_specs=[row(d), pl.BlockSpec((None, d, d_in), lambda i: (layer, 0, 0), pipeline_mode=pl.Buffered(1)),
                  tab, tab, tab],
        out_specs=[row(3 * d_attn), row(d_four), row(d_conv2)],
        out_shape=[jax.ShapeDtypeStruct((n, 3 * d_attn), F32),
                   jax.ShapeDtypeStruct((n, d_four), MXU_DTYPE),
                   jax.ShapeDtypeStruct((n, d_conv2), F32)],
        compiler_params=_params("parallel"),
        name="in_proj",
    )(xb, w_in, *rope)


def _attn_kernel(q_ref, k_ref, v_ref, bias_ref, o_ref, s_scr, p_scr, m_scr, osc, msc, dsc, *, seq):
    lane = lax.broadcasted_iota(jnp.int32, (ATTN_BLOCK, LANES), 1)
    head_a = lane < HEAD_DIM
    scale = HEAD_DIM ** -0.5
    blocks_per_super = ATTN_SUPER // ATTN_BLOCK
    rows2 = 2 * ATTN_BLOCK

    def strided(start, size, d):
        return pl.ds(start, size) if d == 1 else pl.ds(start, size, stride=d)

    def super_block(sb, carry):
        for bi, d in enumerate(DILATIONS):
            sub_len = seq // d
            kw = min(KEY_WINDOW, sub_len)
            per_res = blocks_per_super // d

            def place(j, d=d, sub_len=sub_len, kw=kw, per_res=per_res):
                r = j % d
                bl = j // d
                i = sb * per_res + bl
                ks = jnp.clip(ATTN_BLOCK * (i - 1), 0, sub_len - kw)
                return r, bl, i, ks

            def scores(j, d=d, kw=kw, place=place):
                r, bl, i, ks = place(j)
                q = q_ref[strided(r + d * ATTN_BLOCK * i, ATTN_BLOCK, d), :] * scale
                kk = k_ref[strided(r + d * ks, kw, d), :]
                zero = jnp.zeros_like(q)
                q2 = jnp.concatenate([jnp.where(head_a, q, zero), jnp.where(head_a, zero, q)], axis=0)
                s_scr[j, :, :kw] = lax.dot_general(q2.astype(MXU_DTYPE), kk.astype(MXU_DTYPE),
                                                   (((1,), (1,)), ((), ())), preferred_element_type=F32)

            def softmax(j, kw=kw, place=place):
                r, bl, i, ks = place(j)
                case = (ATTN_BLOCK * i - ks) // ATTN_BLOCK
                for rc in range(0, rows2, SOFTMAX_ROWS):
                    s = jnp.minimum(s_scr[j, rc:rc + SOFTMAX_ROWS, :kw], bias_ref[case, rc:rc + SOFTMAX_ROWS, :kw])
                    m = jnp.max(s, axis=-1, keepdims=True)
                    p_scr[j, rc:rc + SOFTMAX_ROWS, :kw] = jnp.exp(s - m).astype(p_scr.dtype)
                    m_scr[j, rc:rc + SOFTMAX_ROWS, :] = jnp.broadcast_to(m, (SOFTMAX_ROWS, LANES))

            def values(j, d=d, bi=bi, kw=kw, place=place):
                r, bl, i, ks = place(j)
                vv = v_ref[strided(r + d * ks, kw, d), :].astype(MXU_DTYPE)
                vext = jnp.concatenate([vv, jnp.ones_like(vv)], axis=1)
                od = jnp.dot(p_scr[j, :, :kw], vext, preferred_element_type=F32)
                dst = strided(r + d * ATTN_BLOCK * bl, ATTN_BLOCK, d)
                osc[bi, dst, :] = jnp.where(head_a, od[:ATTN_BLOCK, :LANES], od[ATTN_BLOCK:, :LANES])
                dsc[bi, dst, :] = jnp.where(head_a, od[:ATTN_BLOCK, LANES:], od[ATTN_BLOCK:, LANES:])
                msc[bi, dst, :] = jnp.where(head_a, m_scr[j, :ATTN_BLOCK, :], m_scr[j, ATTN_BLOCK:, :])

            for t in range(blocks_per_super + 2):
                if t < blocks_per_super:
                    scores(t)
                if 0 <= t - 1 < blocks_per_super:
                    softmax(t - 1)
                if 0 <= t - 2 < blocks_per_super:
                    values(t - 2)

        def mix(rc, carry2):
            rows = pl.ds(pl.multiple_of(rc * MIX_ROWS, MIX_ROWS), MIX_ROWS)
            m0, m1, m2 = msc[0, rows, :], msc[1, rows, :], msc[2, rows, :]
            mx = jnp.maximum(jnp.maximum(m0, m1), m2)
            e0, e1, e2 = jnp.exp(m0 - mx), jnp.exp(m1 - mx), jnp.exp(m2 - mx)
            num = e0 * osc[0, rows, :] + e1 * osc[1, rows, :] + e2 * osc[2, rows, :]
            den = e0 * dsc[0, rows, :] + e1 * dsc[1, rows, :] + e2 * dsc[2, rows, :]
            out_rows = pl.ds(pl.multiple_of(sb * ATTN_SUPER + rc * MIX_ROWS, MIX_ROWS), MIX_ROWS)
            o_ref[out_rows, :] = (num / den).astype(o_ref.dtype)
            return carry2

        lax.fori_loop(0, ATTN_SUPER // MIX_ROWS, mix, 0, unroll=2)
        return carry

    lax.fori_loop(0, seq // ATTN_SUPER, super_block, 0)


def _attn_bias():
    row = jnp.arange(2 * ATTN_BLOCK, dtype=jnp.int32)[:, None] % ATTN_BLOCK
    col = jnp.arange(KEY_PAD, dtype=jnp.int32)[None, :]
    off = jnp.arange(3, dtype=jnp.int32)[:, None, None] * ATTN_BLOCK
    ok = jnp.abs(row[None] - col[None] + off) <= ATTN_RADIUS
    return jnp.where(ok, jnp.inf, NEG_INF).astype(F32)


def _attention(qkv, bias, batch, seq, d_attn):
    pairs = d_attn // LANES
    qkv3 = qkv.reshape(batch, seq, 3 * d_attn)
    spec = lambda off: pl.BlockSpec((None, seq, LANES), lambda b, h: (b, 0, off + h))
    blocks_per_super = ATTN_SUPER // ATTN_BLOCK
    out = pl.pallas_call(
        functools.partial(_attn_kernel, seq=seq),
        grid=(batch, pairs),
        in_specs=[spec(0), spec(pairs), spec(2 * pairs),
                  pl.BlockSpec(bias.shape, lambda b, h: (0, 0, 0))],
        out_specs=pl.BlockSpec((None, seq, LANES), lambda b, h: (b, 0, h)),
        out_shape=jax.ShapeDtypeStruct((batch, seq, d_attn), MXU_DTYPE),
        scratch_shapes=[pltpu.VMEM((blocks_per_super, 2 * ATTN_BLOCK, KEY_PAD), F32),
                        pltpu.VMEM((blocks_per_super, 2 * ATTN_BLOCK, KEY_PAD), MXU_DTYPE),
                        pltpu.VMEM((blocks_per_super, 2 * ATTN_BLOCK, LANES), F32),
                        pltpu.VMEM((len(DILATIONS), ATTN_SUPER, LANES), F32),
                        pltpu.VMEM((len(DILATIONS), ATTN_SUPER, LANES), F32),
                        pltpu.VMEM((len(DILATIONS), ATTN_SUPER, LANES), F32)],
        compiler_params=_params("parallel", "parallel"),
        name="dilated_attention",
    )(qkv3, qkv3, qkv3, bias)
    return out.reshape(batch * seq, d_attn)


def _fourier_kernel(cs_ref, ss_ref, u_ref, cc_ref, o_ref, acc_ref, *, d_four):
    k = pl.program_id(2)

    @pl.when(k == 0)
    def _():
        acc_ref[...] = jnp.zeros_like(acc_ref)

    u = u_ref[...]
    acc_ref[:, :d_four] += jnp.dot(cs_ref[...], u, preferred_element_type=F32)
    acc_ref[:, d_four:] += jnp.dot(ss_ref[...], u, preferred_element_type=F32)

    @pl.when(k == pl.num_programs(2) - 1)
    def _():
        y = jnp.dot(acc_ref[...].astype(MXU_DTYPE), cc_ref[...], preferred_element_type=F32)
        o_ref[...] = y.astype(o_ref.dtype)


def _dft_matrices(seq, d_four):
    k = jnp.arange(seq, dtype=jnp.int32)
    ang = ((k[:, None] * k[None, :]) % seq).astype(F32) * (2.0 * math.pi / seq)
    cs = (jnp.cos(ang) * seq ** -0.5).astype(MXU_DTYPE)
    ss = (jnp.sin(ang) * seq ** -0.5).astype(MXU_DTYPE)
    gw = d_four // FOURIER_GROUPS
    c = jnp.arange(gw, dtype=jnp.int32)
    angc = ((c[:, None] * c[None, :]) % gw).astype(F32) * (2.0 * math.pi / gw)
    eye = jnp.eye(FOURIER_GROUPS, dtype=F32)
    cc = jnp.kron(eye, jnp.cos(angc) * gw ** -0.5)
    sc = jnp.kron(eye, jnp.sin(angc) * gw ** -0.5)
    chan = jnp.concatenate([cc, -sc], axis=0).astype(MXU_DTYPE)
    return cs, ss, chan


def _fourier(uf, dft, batch, seq):
    cs, ss, chan = dft
    d_four = uf.shape[1]
    t = min(DFT_TILE, seq)
    u3 = uf.reshape(batch, seq, d_four)
    mat = pl.BlockSpec((t, t), lambda b, i, k: (i, k))
    out = pl.pallas_call(
        functools.partial(_fourier_kernel, d_four=d_four),
        grid=(batch, seq // t, seq // t),
        in_specs=[mat, mat,
                  pl.BlockSpec((None, t, d_four), lambda b, i, k: (b, k, 0)),
                  pl.BlockSpec((2 * d_four, d_four), lambda b, i, k: (0, 0))],
        out_specs=pl.BlockSpec((None, t, d_four), lambda b, i, k: (b, i, 0)),
        out_shape=jax.ShapeDtypeStruct((batch, seq, d_four), MXU_DTYPE),
        scratch_shapes=[pltpu.VMEM((t, 2 * d_four), F32)],
        compiler_params=_params("parallel", "parallel", "arbitrary"),
        name="fourier_mix",
    )(cs, ss, u3, chan)
    return out.reshape(batch * seq, d_four)


def _conv_kernel(cur_ref, prev_ref, next_ref, w_ref, b_ref, g_ref, beta_ref, o_ref, hp_ref, acc_ref, *, d_conv):
    i = pl.program_id(1)
    last = pl.num_programs(1) - 1
    tile = cur_ref.shape[0]

    def glu(ref):
        return ref[:, :d_conv] * jax.nn.sigmoid(ref[:, d_conv:])

    hp_ref[pl.ds(0, CONV_HALO), :] = jnp.where(i > 0, glu(prev_ref), 0.0)
    hp_ref[pl.ds(CONV_HALO, tile), :] = glu(cur_ref)
    hp_ref[pl.ds(CONV_HALO + tile, CONV_HALO), :] = jnp.where(i < last, glu(next_ref), 0.0)

    first_tap = CONV_HALO - CONV_PAD
    for c in range(0, d_conv, LANES):
        taps = [w_ref[pl.ds(j, 1), c:c + LANES] for j in range(CONV_WIDTH)]

        def rows(rc, carry, c=c, taps=taps):
            base = pl.multiple_of(rc * CONV_ROWS, CONV_ROWS)
            win = hp_ref[pl.ds(base, CONV_ROWS + 2 * CONV_HALO), c:c + LANES]
            acc = jnp.zeros((CONV_ROWS, LANES), F32)
            for sh in range(SUBLANES):
                part = None
                for j in range(sh, CONV_WIDTH, SUBLANES):
                    term = win[j - sh:j - sh + CONV_ROWS + SUBLANES] * taps[j]
                    part = term if part is None else part + term
                acc = acc + part[first_tap + sh:first_tap + sh + CONV_ROWS]
            acc_ref[pl.ds(base, CONV_ROWS), c:c + LANES] = acc
            return carry

        lax.fori_loop(0, tile // CONV_ROWS, rows, 0)

    h = _layer_norm(acc_ref[...] + b_ref[...], g_ref[...], beta_ref[...])
    o_ref[...] = (h * jax.nn.sigmoid(h)).astype(o_ref.dtype)


def _conv_module(uc, conv_w, conv_b, ln_g, ln_b, batch, seq):
    d_conv = uc.shape[1] // 2
    tile = ROW_TILE
    halo_per_tile = tile // CONV_HALO
    n_halo = seq // CONV_HALO
    u3 = uc.reshape(batch, seq, 2 * d_conv)
    vec = pl.BlockSpec((1, d_conv), lambda b, i: (0, 0))
    out = pl.pallas_call(
        functools.partial(_conv_kernel, d_conv=d_conv),
        grid=(batch, seq // tile),
        in_specs=[pl.BlockSpec((None, tile, 2 * d_conv), lambda b, i: (b, i, 0)),
                  pl.BlockSpec((None, CONV_HALO, 2 * d_conv),
                               lambda b, i: (b, jnp.maximum(i * halo_per_tile - 1, 0), 0)),
                  pl.BlockSpec((None, CONV_HALO, 2 * d_conv),
                               lambda b, i: (b, jnp.minimum((i + 1) * halo_per_tile, n_halo - 1), 0)),
                  pl.BlockSpec((CONV_WIDTH, d_conv), lambda b, i: (0, 0)),
                  vec, vec, vec],
        out_specs=pl.BlockSpec((None, tile, d_conv), lambda b, i: (b, i, 0)),
        out_shape=jax.ShapeDtypeStruct((batch, seq, d_conv), MXU_DTYPE),
        scratch_shapes=[pltpu.VMEM((tile + 2 * CONV_HALO, d_conv), F32), pltpu.VMEM((tile, d_conv), F32)],
        compiler_params=_params("parallel", "parallel"),
        name="conv_module",
    )(u3, u3, u3, conv_w, conv_b.reshape(1, d_conv), ln_g.reshape(1, d_conv), ln_b.reshape(1, d_conv))
    return out.reshape(batch * seq, d_conv)


def _out_proj_kernel(yf_ref, ya_ref, yc_ref, x_ref, w_ref, g_ref, b_ref, wr_ref, xf_ref, xb_ref, aff_ref, *, alpha):
    d_four, d_attn = yf_ref.shape[1], ya_ref.shape[1]
    mix = jnp.dot(yf_ref[...], w_ref[:d_four, :], preferred_element_type=F32)
    mix += jnp.dot(ya_ref[...], w_ref[d_four:d_four + d_attn, :], preferred_element_type=F32)
    mix += jnp.dot(yc_ref[...], w_ref[d_four + d_attn:, :], preferred_element_type=F32)
    x1 = _layer_norm(alpha * x_ref[...] + mix, g_ref[...], b_ref[...])
    xf_ref[...] = x1
    xb = x1.astype(xb_ref.dtype)
    xb_ref[...] = xb
    logits = jnp.dot(xb, wr_ref[...], preferred_element_type=F32)
    lt = logits.T[:N_EXPERTS, :]
    e = jnp.exp(lt - jnp.max(lt, axis=0, keepdims=True))
    aff_ref[...] = e / jnp.sum(e, axis=0, keepdims=True)


def _out_proj(yf, ya, yc, xf, w_out, layer, g, b, w_router_pad, alpha):
    n, d = xf.shape
    row = lambda w: pl.BlockSpec((ROW_TILE, w), lambda i: (i, 0))
    vec = pl.BlockSpec((1, d), lambda i: (0, 0))
    return pl.pallas_call(
        functools.partial(_out_proj_kernel, alpha=alpha),
        grid=(n // ROW_TILE,),
        in_specs=[row(yf.shape[1]), row(ya.shape[1]), row(yc.shape[1]), row(d),
                  pl.BlockSpec((None,) + w_out.shape[1:], lambda i: (layer, 0, 0), pipeline_mode=pl.Buffered(1)),
                  vec, vec,
                  pl.BlockSpec((None,) + w_router_pad.shape[1:], lambda i: (layer, 0, 0),
                               pipeline_mode=pl.Buffered(1))],
        out_specs=[row(d), row(d), pl.BlockSpec((N_EXPERTS, ROW_TILE), lambda i: (0, i))],
        out_shape=[jax.ShapeDtypeStruct((n, d), F32), jax.ShapeDtypeStruct((n, d), MXU_DTYPE),
                   jax.ShapeDtypeStruct((N_EXPERTS, n), F32)],
        compiler_params=_params("parallel"),
        name="out_proj_ln_router",
    )(yf, ya, yc, xf, w_out, g.reshape(1, d), b.reshape(1, d), w_router_pad)


def _ffn_kernel(x_ref, wg_ref, wu_ref, wd_ref, gate_ref, o_ref, acc_ref):
    f = pl.program_id(2)
    x = x_ref[...]
    g = jnp.dot(x, wg_ref[...], preferred_element_type=F32)
    u = jnp.dot(x, wu_ref[...], preferred_element_type=F32)
    hid = (g * jax.nn.sigmoid(g) * u).astype(MXU_DTYPE)
    part = jnp.dot(hid, wd_ref[...], preferred_element_type=F32)

    @pl.when(f == 0)
    def _():
        acc_ref[...] = part

    @pl.when(f > 0)
    def _():
        acc_ref[...] += part

    @pl.when(f == pl.num_programs(2) - 1)
    def _():
        o_ref[...] = acc_ref[...] * gate_ref[...]


def _expert_ffn(xe, w_gate, w_up, w_down, layer, gate, cap):
    rows, d = xe.shape
    _, n_exp, _, d_hid = w_gate.shape
    tm = min(FFN_ROW_TILE, cap)
    tf = FFN_HID_TILE
    tiles = cap // tm
    return pl.pallas_call(
        _ffn_kernel,
        grid=(n_exp, tiles, d_hid // tf),
        in_specs=[pl.BlockSpec((tm, d), lambda e, i, f: (e * tiles + i, 0)),
                  pl.BlockSpec((None, None, d, tf), lambda e, i, f: (layer, e, 0, f)),
                  pl.BlockSpec((None, None, d, tf), lambda e, i, f: (layer, e, 0, f)),
                  pl.BlockSpec((None, None, tf, d), lambda e, i, f: (layer, e, f, 0)),
                  pl.BlockSpec((tm, 1), lambda e, i, f: (e * tiles + i, 0))],
        out_specs=pl.BlockSpec((tm, d), lambda e, i, f: (e * tiles + i, 0)),
        out_shape=jax.ShapeDtypeStruct((rows, d), F32),
        scratch_shapes=[pltpu.VMEM((tm, d), F32)],
        compiler_params=_params("parallel", "parallel", "arbitrary"),
        name="expert_ffn",
    )(xe, w_gate, w_up, w_down, gate)


def _combine_ln_kernel(off_ref, x_ref, g_ref, b_ref, tok_hbm, rows_hbm, xf_ref, xb_ref,
                       tok_buf, row_buf, acc_ref, sem, *, alpha):
    blk = pl.program_id(0)
    n_blk = pl.num_programs(0)
    tm, d = x_ref.shape

    def first_row(b):
        return (off_ref[b] // SEG_ALIGN) * SEG_ALIGN

    def copies(b, c, slot):
        s = pl.multiple_of(first_row(b) + c * SEG_CHUNK, SEG_ALIGN)
        return (pltpu.make_async_copy(tok_hbm.at[:, pl.ds(s, SEG_CHUNK)], tok_buf.at[slot], sem.at[0, slot]),
                pltpu.make_async_copy(rows_hbm.at[pl.ds(s, SEG_CHUNK), :], row_buf.at[slot], sem.at[1, slot]))

    def start(b, c, slot):
        for cp in copies(b, c, slot):
            cp.start()

    def wait(b, c, slot):
        for cp in copies(b, c, slot):
            cp.wait()

    @pl.when(blk == 0)
    def _():
        start(blk, 0, 0)

    n_chunks = jnp.maximum((off_ref[blk + 1] - first_row(blk) + SEG_CHUNK - 1) // SEG_CHUNK, 1)
    acc_ref[...] = jnp.zeros_like(acc_ref)
    token = lax.broadcasted_iota(jnp.int32, (tm, SEG_CHUNK), 0) + blk * tm

    def chunk(c, carry):
        slot = c % 2
        wait(blk, c, slot)

        @pl.when(c + 1 < n_chunks)
        def _():
            start(blk, c + 1, 1 - slot)

        onehot = (tok_buf[slot] == token).astype(MXU_DTYPE)
        for dc in range(0, d, SEG_COLS):
            rows = row_buf[slot, :, dc:dc + SEG_COLS]
            hi = rows.astype(MXU_DTYPE)
            lo = (rows - hi.astype(F32)).astype(MXU_DTYPE)
            acc_ref[:, dc:dc + SEG_COLS] += (jnp.dot(onehot, hi, preferred_element_type=F32)
                                             + jnp.dot(onehot, lo, preferred_element_type=F32))
        return carry

    lax.fori_loop(0, n_chunks, chunk, 0)

    @pl.when(blk + 1 < n_blk)
    def _():
        start(blk + 1, 0, 0)

    x2 = _layer_norm(alpha * x_ref[...] + acc_ref[...], g_ref[...], b_ref[...])
    xf_ref[...] = x2
    xb_ref[...] = x2.astype(xb_ref.dtype)


def _combine_ln(xf, offsets, tok_sorted, rows_sorted, g, b, alpha):
    n, d = xf.shape
    tm = SEG_TOKENS
    row = lambda: pl.BlockSpec((tm, d), lambda i, off: (i, 0))
    vec = lambda: pl.BlockSpec((1, d), lambda i, off: (0, 0))
    hbm = lambda: pl.BlockSpec(memory_space=pl.ANY)
    return pl.pallas_call(
        functools.partial(_combine_ln_kernel, alpha=alpha),
        grid_spec=pltpu.PrefetchScalarGridSpec(
            num_scalar_prefetch=1,
            grid=(n // tm,),
            in_specs=[row(), vec(), vec(), hbm(), hbm()],
            out_specs=[row(), row()],
            scratch_shapes=[pltpu.VMEM((2, 1, SEG_CHUNK), jnp.int32),
                            pltpu.VMEM((2, SEG_CHUNK, d), F32),
                            pltpu.VMEM((tm, d), F32),
                            pltpu.SemaphoreType.DMA((2, 2))]),
        out_shape=[jax.ShapeDtypeStruct((n, d), F32), jax.ShapeDtypeStruct((n, d), MXU_DTYPE)],
        compiler_params=_params("arbitrary"),
        name="moe_combine_ln",
    )(offsets, xf, g.reshape(1, d), b.reshape(1, d), tok_sorted, rows_sorted)


def _moe_combine_ln(xf, xb, aff_t, w_gate, w_up, w_down, layer, g, b, alpha):
    n, d = xf.shape
    cap = max(1, CAPACITY_FACTOR * n // N_EXPERTS)
    gate, idx = lax.top_k(aff_t, cap)
    flat = idx.reshape(-1)
    xe = jnp.take(xb, flat, axis=0)
    ye = _expert_ffn(xe, w_gate, w_up, w_down, layer, gate.reshape(-1, 1), cap)
    tok_sorted, order = lax.sort_key_val(flat, lax.iota(jnp.int32, flat.shape[0]))
    tok_sorted = jnp.pad(tok_sorted, (0, SEG_CHUNK), constant_values=-1)
    order = jnp.pad(order, (0, SEG_CHUNK))
    rows_sorted = jnp.take(ye, order, axis=0)
    bounds = jnp.arange(n // SEG_TOKENS + 1, dtype=jnp.int32) * SEG_TOKENS
    offsets = jnp.searchsorted(tok_sorted[:flat.shape[0]], bounds, side="left").astype(jnp.int32)
    return _combine_ln(xf, offsets, tok_sorted.reshape(1, -1), rows_sorted, g, b, alpha)


def _trunk(x, weights, depth, alpha):
    batch, seq, d = x.shape
    n = batch * seq
    d_attn = (3 * d) // 8
    d_four = d // 4
    rope = _rope_tables(seq)
    dft = _dft_matrices(seq, d_four)
    bias = _attn_bias()
    xf, xb = _emb_ln(x.reshape(n, d), weights["emb_ln_g"], weights["emb_ln_b"])
    big = ("w_in", "w_out", "w_router", "w_gate", "w_up", "w_down")
    for l in range(depth):
        w = {k: v[l] for k, v in weights.items() if not k.startswith("emb_") and k not in big}
        qkv, uf, uc = _in_proj(xb, weights["w_in"], l, rope, seq, d_attn, d_four)
        ya = _attention(qkv, bias, batch, seq, d_attn)
        yf = _fourier(uf, dft, batch, seq)
        yc = _conv_module(uc, w["conv_w"], w["conv_b"], w["conv_ln_g"], w["conv_ln_b"], batch, seq)
        x1f, x1b, aff_t = _out_proj(yf, ya, yc, xf, weights["w_out"], l, w["ln1_g"], w["ln1_b"],
                                    weights["w_router"], alpha)
        xf, xb = _moe_combine_ln(x1f, x1b, aff_t, weights["w_gate"], weights["w_up"], weights["w_down"], l,
                                 w["ln2_g"], w["ln2_b"], alpha)
    return xf.reshape(batch, seq, d)


def kernel(x_prompt, x_sample, emb_ln_g, emb_ln_b, w_in, conv_w, conv_b, conv_ln_g, conv_ln_b, w_out, ln1_g, ln1_b, w_router, w_gate, w_up, w_down, ln2_g, ln2_b):
    depth = w_in.shape[0]
    alpha = (2 * depth) ** 0.25
    w_router_pad = jnp.pad(w_router, ((0, 0), (0, 0), (0, LANES - w_router.shape[-1])))
    weights = dict(emb_ln_g=emb_ln_g, emb_ln_b=emb_ln_b, conv_w=conv_w, conv_b=conv_b, conv_ln_g=conv_ln_g,
                   conv_ln_b=conv_ln_b, ln1_g=ln1_g, ln1_b=ln1_b, ln2_g=ln2_g, ln2_b=ln2_b,
                   w_in=w_in.astype(MXU_DTYPE), w_out=w_out.astype(MXU_DTYPE),
                   w_router=w_router_pad.astype(MXU_DTYPE), w_gate=w_gate.astype(MXU_DTYPE),
                   w_up=w_up.astype(MXU_DTYPE), w_down=w_down.astype(MXU_DTYPE))
    y_prompt = _trunk(x_prompt, weights, depth, alpha)
    y_sample = _trunk(x_sample, weights, depth, alpha)
    return (y_prompt, y_sample)
```

```python
import functools
import math

import jax
import jax.numpy as jnp
from jax import lax
from jax.experimental import pallas as pl
from jax.experimental.pallas import tpu as pltpu

F32 = jnp.float32
MXU_DTYPE = jnp.bfloat16

HEAD_DIM = 64
FOURIER_GROUPS = 4
CONV_WIDTH = 31
CONV_PAD = CONV_WIDTH // 2
DILATIONS = (1, 4, 16)
ATTN_RADIUS = 64
ATTN_BLOCK = 64
ROPE_THETA = 500000.0
ROPE_DIM = HEAD_DIM // 4
N_EXPERTS = 16
CAPACITY_FACTOR = 2
LN_EPS = 1e-5
NEG_INF = -1e30
LOG2_E = math.log2(math.e)

LANES = 128
SUBLANES = 8
VMEM_LIMIT = 56 * 1024 * 1024

ROW_TILE = 512
ATTN_SUPER = ATTN_BLOCK * max(DILATIONS)
KEY_WINDOW = 3 * ATTN_BLOCK
KEY_PAD = 2 * LANES
SOFTMAX_ROWS = 32
MIX_ROWS = 64
ATTN_SKEW = 3
CONV_HALO = 16
CONV_ROWS = 64
OUT_SUB_ROWS = 256
FFN_ROW_TILE = 1024
FFN_HID_TILE = 512
FFN_SUB_ROWS = 512
SEG_TOKENS = 256
SEG_CHUNK = 256
SEG_ALIGN = 128
SEG_COLS = 512
DFT_P = 128
DFT_LANES = 8192
DFT_CHUNK = 1024
DFT_GROUP = 8


def _params(*sem):
    return pltpu.CompilerParams(dimension_semantics=sem, vmem_limit_bytes=VMEM_LIMIT)


def _layer_norm(x, g, b):
    mu = jnp.mean(x, axis=-1, keepdims=True)
    xc = x - mu
    var = jnp.mean(xc * xc, axis=-1, keepdims=True)
    return xc * lax.rsqrt(var + LN_EPS) * g + b


def _emb_ln_kernel(x_ref, g_ref, b_ref, xf_ref, xb_ref):
    y = _layer_norm(x_ref[...], g_ref[...], b_ref[...])
    xf_ref[...] = y
    xb_ref[...] = y.astype(xb_ref.dtype)


def _emb_ln(x, g, b):
    n, d = x.shape
    row = pl.BlockSpec((ROW_TILE, d), lambda i: (i, 0))
    vec = pl.BlockSpec((1, d), lambda i: (0, 0))
    return pl.pallas_call(
        _emb_ln_kernel,
        grid=(n // ROW_TILE,),
        in_specs=[row, vec, vec],
        out_specs=[row, row],
        out_shape=[jax.ShapeDtypeStruct((n, d), F32), jax.ShapeDtypeStruct((n, d), MXU_DTYPE)],
        compiler_params=_params("parallel"),
        name="emb_ln",
    )(x, g.reshape(1, d), b.reshape(1, d))


def _in_proj_kernel(x_ref, w_ref, cos_ref, sa_ref, sb_ref, qkv_ref, uf_ref, uc_ref, *, d_attn, d_four):
    x = x_ref[...]
    cos, sa, sb = cos_ref[...], sa_ref[...], sb_ref[...]
    half = ROPE_DIM // 2
    chunk = 2 * LANES
    d_qkv = 3 * d_attn
    for c in range(0, d_qkv, chunk):
        acc = jnp.dot(x, w_ref[:, c:c + chunk], preferred_element_type=F32)
        if c < 2 * d_attn:
            for h in range(0, chunk, LANES):
                a = acc[:, h:h + LANES]
                rot = a * cos + pltpu.roll(a, LANES - half, 1) * sa + pltpu.roll(a, half, 1) * sb
                qkv_ref[:, c + h:c + h + LANES] = rot
        else:
            qkv_ref[:, c:c + chunk] = acc
    for c in range(0, d_four, chunk):
        acc = jnp.dot(x, w_ref[:, d_qkv + c:d_qkv + c + chunk], preferred_element_type=F32)
        uf_ref[:, c:c + chunk] = acc.astype(uf_ref.dtype)
    d_conv2 = uc_ref.shape[1]
    for c in range(0, d_conv2, chunk):
        off = d_qkv + d_four + c
        uc_ref[:, c:c + chunk] = jnp.dot(x, w_ref[:, off:off + chunk], preferred_element_type=F32)


def _rope_tables(seq):
    half = ROPE_DIM // 2
    pos = jnp.arange(seq, dtype=F32)
    inv_freq = ROPE_THETA ** (-jnp.arange(0, ROPE_DIM, 2, dtype=F32) / ROPE_DIM)
    ang = pos[:, None] * inv_freq[None, :]
    cos, sin = jnp.cos(ang), jnp.sin(ang)
    rest = HEAD_DIM - ROPE_DIM
    ones = jnp.ones((seq, rest), F32)
    zeros = jnp.zeros((seq, rest), F32)
    zh = jnp.zeros((seq, half), F32)
    cos_h = jnp.concatenate([cos, cos, ones], axis=1)
    sa_h = jnp.concatenate([-sin, zh, zeros], axis=1)
    sb_h = jnp.concatenate([zh, sin, zeros], axis=1)
    reps = LANES // HEAD_DIM
    return tuple(jnp.tile(t, (1, reps)) for t in (cos_h, sa_h, sb_h))


def _in_proj(xb, w_in, layer, rope, seq, d_attn, d_four):
    n, d = xb.shape
    d_in = w_in.shape[2]
    d_conv2 = d_in - 3 * d_attn - d_four
    tiles_per_seq = seq // ROW_TILE
    row = lambda w: pl.BlockSpec((ROW_TILE, w), lambda i: (i, 0))
    tab = pl.BlockSpec((ROW_TILE, LANES), lambda i: (i % tiles_per_seq, 0))
    return pl.pallas_call(
        functools.partial(_in_proj_kernel, d_attn=d_attn, d_four=d_four),
        grid=(n // ROW_TILE,),
        in_specs=[row(d), pl.BlockSpec((None, d, d_in), lambda i: (layer, 0, 0), pipeline_mode=pl.Buffered(1)),
                  tab, tab, tab],
        out_specs=[row(3 * d_attn), row(d_four), row(d_conv2)],
        out_shape=[jax.ShapeDtypeStruct((n, 3 * d_attn), F32),
                   jax.ShapeDtypeStruct((n, d_four), MXU_DTYPE),
                   jax.ShapeDtypeStruct((n, d_conv2), F32)],
        compiler_params=_params("parallel"),
        name="in_proj",
    )(xb, w_in, *rope)


def _attn_kernel(q_ref, k_ref, v_ref, bias_ref, o_ref, s_scr, p_scr, m_scr, osc, msc, dsc, *, seq):
    lane = lax.broadcasted_iota(jnp.int32, (ATTN_BLOCK, LANES), 1)
    head_a = lane < HEAD_DIM
    scale = HEAD_DIM ** -0.5 * LOG2_E
    blocks_per_super = ATTN_SUPER // ATTN_BLOCK
    rows2 = 2 * ATTN_BLOCK

    def strided(start, size, d):
        return pl.ds(start, size) if d == 1 else pl.ds(start, size, stride=d)

    def super_block(sb, carry):
        for bi, d in enumerate(DILATIONS):
            sub_len = seq // d
            kw = min(KEY_WINDOW, sub_len)
            per_res = blocks_per_super // d

            def place(j, d=d, sub_len=sub_len, kw=kw, per_res=per_res):
                r = j % d
                bl = j // d
                i = sb * per_res + bl
                ks = jnp.clip(ATTN_BLOCK * (i - 1), 0, sub_len - kw)
                return r, bl, i, ks

            def scores(j, d=d, kw=kw, place=place):
                r, bl, i, ks = place(j)
                q = q_ref[strided(r + d * ATTN_BLOCK * i, ATTN_BLOCK, d), :] * scale
                kk = k_ref[strided(r + d * ks, kw, d), :]
                zero = jnp.zeros_like(q)
                q2 = jnp.concatenate([jnp.where(head_a, q, zero), jnp.where(head_a, zero, q)], axis=0)
                s_scr[j, :, :kw] = lax.dot_general(q2.astype(MXU_DTYPE), kk.astype(MXU_DTYPE),
                                                   (((1,), (1,)), ((), ())), preferred_element_type=F32)

            def softmax(j, kw=kw, place=place):
                r, bl, i, ks = place(j)
                case = (ATTN_BLOCK * i - ks) // ATTN_BLOCK
                for rc in range(0, rows2, SOFTMAX_ROWS):
                    s = jnp.minimum(s_scr[j, rc:rc + SOFTMAX_ROWS, :kw], bias_ref[case, rc:rc + SOFTMAX_ROWS, :kw])
                    m = jnp.max(s, axis=-1, keepdims=True)
                    p_scr[j, rc:rc + SOFTMAX_ROWS, :kw] = jnp.exp2(s - m).astype(p_scr.dtype)
                    m_scr[j, rc:rc + SOFTMAX_ROWS, :] = jnp.broadcast_to(m, (SOFTMAX_ROWS, LANES))

            def values(j, d=d, bi=bi, kw=kw, place=place):
                r, bl, i, ks = place(j)
                vv = v_ref[strided(r + d * ks, kw, d), :].astype(MXU_DTYPE)
                vext = jnp.concatenate([vv, jnp.ones_like(vv)], axis=1)
                od = jnp.dot(p_scr[j, :, :kw], vext, preferred_element_type=F32)
                dst = strided(r + d * ATTN_BLOCK * bl, ATTN_BLOCK, d)
                osc[bi, dst, :] = jnp.where(head_a, od[:ATTN_BLOCK, :LANES], od[ATTN_BLOCK:, :LANES])
                dsc[bi, dst, :] = jnp.where(head_a, od[:ATTN_BLOCK, LANES:], od[ATTN_BLOCK:, LANES:])
                msc[bi, dst, :] = jnp.where(head_a, m_scr[j, :ATTN_BLOCK, :], m_scr[j, ATTN_BLOCK:, :])

            for t in range(blocks_per_super + 2 * ATTN_SKEW):
                if t < blocks_per_super:
                    scores(t)
                if 0 <= t - ATTN_SKEW < blocks_per_super:
                    softmax(t - ATTN_SKEW)
                if 0 <= t - 2 * ATTN_SKEW < blocks_per_super:
                    values(t - 2 * ATTN_SKEW)

        def mix(rc, carry2):
            rows = pl.ds(pl.multiple_of(rc * MIX_ROWS, MIX_ROWS), MIX_ROWS)
            m0, m1, m2 = msc[0, rows, :], msc[1, rows, :], msc[2, rows, :]
            mx = jnp.maximum(jnp.maximum(m0, m1), m2)
            e0, e1, e2 = jnp.exp2(m0 - mx), jnp.exp2(m1 - mx), jnp.exp2(m2 - mx)
            num = e0 * osc[0, rows, :] + e1 * osc[1, rows, :] + e2 * osc[2, rows, :]
            den = e0 * dsc[0, rows, :] + e1 * dsc[1, rows, :] + e2 * dsc[2, rows, :]
            out_rows = pl.ds(pl.multiple_of(sb * ATTN_SUPER + rc * MIX_ROWS, MIX_ROWS), MIX_ROWS)
            o_ref[out_rows, :] = (num / den).astype(o_ref.dtype)
            return carry2

        lax.fori_loop(0, ATTN_SUPER // MIX_ROWS, mix, 0, unroll=2)
        return carry

    lax.fori_loop(0, seq // ATTN_SUPER, super_block, 0)


def _attn_bias():
    row = jnp.arange(2 * ATTN_BLOCK, dtype=jnp.int32)[:, None] % ATTN_BLOCK
    col = jnp.arange(KEY_PAD, dtype=jnp.int32)[None, :]
    off = jnp.arange(3, dtype=jnp.int32)[:, None, None] * ATTN_BLOCK
    ok = jnp.abs(row[None] - col[None] + off) <= ATTN_RADIUS
    return jnp.where(ok, jnp.inf, NEG_INF).astype(F32)


def _attention(qkv, bias, batch, seq, d_attn):
    pairs = d_attn // LANES
    qkv3 = qkv.reshape(batch, seq, 3 * d_attn)
    spec = lambda off: pl.BlockSpec((None, seq, LANES), lambda b, h: (b, 0, off + h))
    blocks_per_super = ATTN_SUPER // ATTN_BLOCK
    out = pl.pallas_call(
        functools.partial(_attn_kernel, seq=seq),
        grid=(batch, pairs),
        in_specs=[spec(0), spec(pairs), spec(2 * pairs),
                  pl.BlockSpec(bias.shape, lambda b, h: (0, 0, 0))],
        out_specs=pl.BlockSpec((None, seq, LANES), lambda b, h: (b, 0, h)),
        out_shape=jax.ShapeDtypeStruct((batch, seq, d_attn), MXU_DTYPE),
        scratch_shapes=[pltpu.VMEM((blocks_per_super, 2 * ATTN_BLOCK, KEY_PAD), F32),
                        pltpu.VMEM((blocks_per_super, 2 * ATTN_BLOCK, KEY_PAD), MXU_DTYPE),
                        pltpu.VMEM((blocks_per_super, 2 * ATTN_BLOCK, LANES), F32),
                        pltpu.VMEM((len(DILATIONS), ATTN_SUPER, LANES), F32),
                        pltpu.VMEM((len(DILATIONS), ATTN_SUPER, LANES), F32),
                        pltpu.VMEM((len(DILATIONS), ATTN_SUPER, LANES), F32)],
        compiler_params=_params("parallel", "parallel"),
        name="dilated_attention",
    )(qkv3, qkv3, qkv3, bias)
    return out.reshape(batch * seq, d_attn)


def _dft_stage1_kernel(w_ref, x_ref, tc_ref, ts_ref, re_ref, nim_ref, *, q):
    w = w_ref[...]
    for c in range(0, x_ref.shape[1], DFT_CHUNK):
        r = jnp.dot(w, x_ref[:, c:c + DFT_CHUNK], preferred_element_type=F32)
        cr, sr = r[:q], r[q:]
        tc, ts = tc_ref[:, c:c + DFT_CHUNK], ts_ref[:, c:c + DFT_CHUNK]
        re_ref[:, c:c + DFT_CHUNK] = (cr * tc - sr * ts).astype(re_ref.dtype)
        nim_ref[:, c:c + DFT_CHUNK] = (cr * ts + sr * tc).astype(nim_ref.dtype)


def _dft_stage2_kernel(re_ref, nim_ref, w2_ref, chan_ref, o_ref, *, p):
    for r in range(0, re_ref.shape[0], p):
        t = jnp.concatenate([re_ref[r:r + p, :], nim_ref[r:r + p, :]], axis=0)
        z = jnp.dot(w2_ref[...], t, preferred_element_type=F32)
        zz = jnp.concatenate([z[:p], z[p:]], axis=1).astype(MXU_DTYPE)
        o_ref[r:r + p, :] = jnp.dot(zz, chan_ref[...], preferred_element_type=F32).astype(o_ref.dtype)


def _dft_tables(seq, d_four):
    p = DFT_P
    q = seq // p

    def cos_sin(n_rows, n_cols, period):
        i = jnp.arange(n_rows, dtype=jnp.int32)[:, None] * jnp.arange(n_cols, dtype=jnp.int32)[None, :]
        ang = (i % period).astype(F32) * (2.0 * math.pi / period)
        return jnp.cos(ang), jnp.sin(ang)

    cq, sq = cos_sin(q, q, q)
    w1 = (jnp.concatenate([cq, sq], axis=0) * q ** -0.5).astype(MXU_DTYPE)
    tc, ts = cos_sin(q, p, seq)
    tc, ts = jnp.repeat(tc, d_four, axis=1), jnp.repeat(ts, d_four, axis=1)
    cp, sp = cos_sin(p, p, p)
    w2 = (jnp.block([[cp, -sp], [sp, cp]]) * p ** -0.5).astype(MXU_DTYPE)
    gw = d_four // FOURIER_GROUPS
    cc, sc = cos_sin(gw, gw, gw)
    eye = jnp.eye(FOURIER_GROUPS, dtype=F32)
    chan = (jnp.concatenate([jnp.kron(eye, cc), -jnp.kron(eye, sc)], axis=0) * gw ** -0.5).astype(MXU_DTYPE)
    return w1, tc, ts, w2, chan


def _fourier(uf, dft, batch, seq):
    w1, tc, ts, w2, chan = dft
    d_four = uf.shape[1]
    p = DFT_P
    q = seq // p
    flat = p * d_four
    x3 = uf.reshape(batch, q, flat)
    xspec = pl.BlockSpec((None, q, DFT_LANES), lambda j, b: (b, 0, j))
    tspec = pl.BlockSpec((q, DFT_LANES), lambda j, b: (0, j))
    t_re, t_nim = pl.pallas_call(
        functools.partial(_dft_stage1_kernel, q=q),
        grid=(flat // DFT_LANES, batch),
        in_specs=[pl.BlockSpec((2 * q, q), lambda j, b: (0, 0)), xspec, tspec, tspec],
        out_specs=[xspec, xspec],
        out_shape=[jax.ShapeDtypeStruct((batch, q, flat), MXU_DTYPE)] * 2,
        compiler_params=_params("parallel", "arbitrary"),
        name="fourier_stage1",
    )(w1, x3, tc, ts)
    rows = DFT_GROUP * p
    rspec = pl.BlockSpec((None, rows, d_four), lambda b, i: (b, i, 0))
    y = pl.pallas_call(
        functools.partial(_dft_stage2_kernel, p=p),
        grid=(batch, q // DFT_GROUP),
        in_specs=[rspec, rspec,
                  pl.BlockSpec((2 * p, 2 * p), lambda b, i: (0, 0)),
                  pl.BlockSpec((2 * d_four, d_four), lambda b, i: (0, 0))],
        out_specs=rspec,
        out_shape=jax.ShapeDtypeStruct((batch, seq, d_four), MXU_DTYPE),
        compiler_params=_params("parallel", "parallel"),
        name="fourier_stage2",
    )(t_re.reshape(batch, seq, d_four), t_nim.reshape(batch, seq, d_four), w2, chan)
    y = y.reshape(batch, q, p, d_four).transpose(0, 2, 1, 3)
    return y.reshape(batch * seq, d_four)


def _conv_kernel(cur_ref, prev_ref, next_ref, w_ref, b_ref, g_ref, beta_ref, o_ref, hp_ref, acc_ref, *, d_conv):
    i = pl.program_id(1)
    last = pl.num_programs(1) - 1
    tile = cur_ref.shape[0]

    def glu(ref):
        return ref[:, :d_conv] * jax.nn.sigmoid(ref[:, d_conv:])

    hp_ref[pl.ds(0, CONV_HALO), :] = jnp.where(i > 0, glu(prev_ref), 0.0)
    hp_ref[pl.ds(CONV_HALO, tile), :] = glu(cur_ref)
    hp_ref[pl.ds(CONV_HALO + tile, CONV_HALO), :] = jnp.where(i < last, glu(next_ref), 0.0)

    first_tap = CONV_HALO - CONV_PAD
    for c in range(0, d_conv, LANES):
        taps = [w_ref[pl.ds(j, 1), c:c + LANES] for j in range(CONV_WIDTH)]

        def rows(rc, carry, c=c, taps=taps):
            base = pl.multiple_of(rc * CONV_ROWS, CONV_ROWS)
            win = hp_ref[pl.ds(base, CONV_ROWS + 2 * CONV_HALO), c:c + LANES]
            acc = jnp.zeros((CONV_ROWS, LANES), F32)
            for sh in range(SUBLANES):
                part = None
                for j in range(sh, CONV_WIDTH, SUBLANES):
                    term = win[j - sh:j - sh + CONV_ROWS + SUBLANES] * taps[j]
                    part = term if part is None else part + term
                acc = acc + part[first_tap + sh:first_tap + sh + CONV_ROWS]
            acc_ref[pl.ds(base, CONV_ROWS), c:c + LANES] = acc
            return carry

        lax.fori_loop(0, tile // CONV_ROWS, rows, 0)

    h = _layer_norm(acc_ref[...] + b_ref[...], g_ref[...], beta_ref[...])
    o_ref[...] = (h * jax.nn.sigmoid(h)).astype(o_ref.dtype)


def _conv_module(uc, conv_w, conv_b, ln_g, ln_b, batch, seq):
    d_conv = uc.shape[1] // 2
    tile = ROW_TILE
    halo_per_tile = tile // CONV_HALO
    n_halo = seq // CONV_HALO
    u3 = uc.reshape(batch, seq, 2 * d_conv)
    vec = pl.BlockSpec((1, d_conv), lambda b, i: (0, 0))
    out = pl.pallas_call(
        functools.partial(_conv_kernel, d_conv=d_conv),
        grid=(batch, seq // tile),
        in_specs=[pl.BlockSpec((None, tile, 2 * d_conv), lambda b, i: (b, i, 0)),
                  pl.BlockSpec((None, CONV_HALO, 2 * d_conv),
                               lambda b, i: (b, jnp.maximum(i * halo_per_tile - 1, 0), 0)),
                  pl.BlockSpec((None, CONV_HALO, 2 * d_conv),
                               lambda b, i: (b, jnp.minimum((i + 1) * halo_per_tile, n_halo - 1), 0)),
                  pl.BlockSpec((CONV_WIDTH, d_conv), lambda b, i: (0, 0)),
                  vec, vec, vec],
        out_specs=pl.BlockSpec((None, tile, d_conv), lambda b, i: (b, i, 0)),
        out_shape=jax.ShapeDtypeStruct((batch, seq, d_conv), MXU_DTYPE),
        scratch_shapes=[pltpu.VMEM((tile + 2 * CONV_HALO, d_conv), F32), pltpu.VMEM((tile, d_conv), F32)],
        compiler_params=_params("parallel", "parallel"),
        name="conv_module",
    )(u3, u3, u3, conv_w, conv_b.reshape(1, d_conv), ln_g.reshape(1, d_conv), ln_b.reshape(1, d_conv))
    return out.reshape(batch * seq, d_conv)


def _out_proj_kernel(yf_ref, ya_ref, yc_ref, x_ref, w_ref, g_ref, b_ref, wr_ref, xf_ref, xb_ref, aff_ref, *, alpha):
    d_four, d_attn = yf_ref.shape[1], ya_ref.shape[1]
    for r in range(0, x_ref.shape[0], OUT_SUB_ROWS):
        rows = slice(r, r + OUT_SUB_ROWS)
        mix = jnp.dot(yf_ref[rows, :], w_ref[:d_four, :], preferred_element_type=F32)
        mix += jnp.dot(ya_ref[rows, :], w_ref[d_four:d_four + d_attn, :], preferred_element_type=F32)
        mix += jnp.dot(yc_ref[rows, :], w_ref[d_four + d_attn:, :], preferred_element_type=F32)
        x1 = _layer_norm(alpha * x_ref[rows, :] + mix, g_ref[...], b_ref[...])
        xf_ref[rows, :] = x1
        xb = x1.astype(xb_ref.dtype)
        xb_ref[rows, :] = xb
        logits = jnp.dot(xb, wr_ref[...], preferred_element_type=F32)
        lt = logits.T[:N_EXPERTS, :]
        e = jnp.exp(lt - jnp.max(lt, axis=0, keepdims=True))
        aff_ref[:, rows] = e / jnp.sum(e, axis=0, keepdims=True)


def _out_proj(yf, ya, yc, xf, w_out, layer, g, b, w_router_pad, alpha):
    n, d = xf.shape
    row = lambda w: pl.BlockSpec((ROW_TILE, w), lambda i: (i, 0))
    vec = pl.BlockSpec((1, d), lambda i: (0, 0))
    return pl.pallas_call(
        functools.partial(_out_proj_kernel, alpha=alpha),
        grid=(n // ROW_TILE,),
        in_specs=[row(yf.shape[1]), row(ya.shape[1]), row(yc.shape[1]), row(d),
                  pl.BlockSpec((None,) + w_out.shape[1:], lambda i: (layer, 0, 0), pipeline_mode=pl.Buffered(1)),
                  vec, vec,
                  pl.BlockSpec((None,) + w_router_pad.shape[1:], lambda i: (layer, 0, 0),
                               pipeline_mode=pl.Buffered(1))],
        out_specs=[row(d), row(d), pl.BlockSpec((N_EXPERTS, ROW_TILE), lambda i: (0, i))],
        out_shape=[jax.ShapeDtypeStruct((n, d), F32), jax.ShapeDtypeStruct((n, d), MXU_DTYPE),
                   jax.ShapeDtypeStruct((N_EXPERTS, n), F32)],
        compiler_params=_params("parallel"),
        name="out_proj_ln_router",
    )(yf, ya, yc, xf, w_out, g.reshape(1, d), b.reshape(1, d), w_router_pad)


def _ffn_kernel(x_ref, wg_ref, wu_ref, wd_ref, gate_ref, o_ref, acc_ref):
    f = pl.program_id(2)

    @pl.when(f == 0)
    def _():
        acc_ref[...] = jnp.zeros_like(acc_ref)

    sub = min(FFN_SUB_ROWS, x_ref.shape[0])
    for r in range(0, x_ref.shape[0], sub):
        x = x_ref[r:r + sub, :]
        g = jnp.dot(x, wg_ref[...], preferred_element_type=F32)
        u = jnp.dot(x, wu_ref[...], preferred_element_type=F32)
        hid = (g * jax.nn.sigmoid(g) * u).astype(MXU_DTYPE)
        acc_ref[r:r + sub, :] += jnp.dot(hid, wd_ref[...], preferred_element_type=F32)

    @pl.when(f == pl.num_programs(2) - 1)
    def _():
        o_ref[...] = (acc_ref[...] * gate_ref[...]).astype(o_ref.dtype)


def _expert_ffn(xe, w_gate, w_up, w_down, layer, gate, cap):
    rows, d = xe.shape
    _, n_exp, _, d_hid = w_gate.shape
    tm = min(FFN_ROW_TILE, cap)
    tf = FFN_HID_TILE
    tiles = cap // tm
    return pl.pallas_call(
        _ffn_kernel,
        grid=(n_exp, tiles, d_hid // tf),
        in_specs=[pl.BlockSpec((tm, d), lambda e, i, f: (e * tiles + i, 0)),
                  pl.BlockSpec((None, None, d, tf), lambda e, i, f: (layer, e, 0, f)),
                  pl.BlockSpec((None, None, d, tf), lambda e, i, f: (layer, e, 0, f)),
                  pl.BlockSpec((None, None, tf, d), lambda e, i, f: (layer, e, f, 0)),
                  pl.BlockSpec((tm, 1), lambda e, i, f: (e * tiles + i, 0))],
        out_specs=pl.BlockSpec((tm, d), lambda e, i, f: (e * tiles + i, 0)),
        out_shape=jax.ShapeDtypeStruct((rows, d), MXU_DTYPE),
        scratch_shapes=[pltpu.VMEM((tm, d), F32)],
        compiler_params=_params("parallel", "parallel", "arbitrary"),
        name="expert_ffn",
    )(xe, w_gate, w_up, w_down, gate)


def _combine_ln_kernel(off_ref, x_ref, g_ref, b_ref, tok_hbm, rows_hbm, xf_ref, xb_ref,
                       tok_buf, row_buf, acc_ref, sem, *, alpha):
    blk = pl.program_id(0)
    n_blk = pl.num_programs(0)
    tm, d = x_ref.shape

    def first_row(b):
        return (off_ref[b] // SEG_ALIGN) * SEG_ALIGN

    def copies(b, c, slot):
        s = pl.multiple_of(first_row(b) + c * SEG_CHUNK, SEG_ALIGN)
        return (pltpu.make_async_copy(tok_hbm.at[:, pl.ds(s, SEG_CHUNK)], tok_buf.at[slot], sem.at[0, slot]),
                pltpu.make_async_copy(rows_hbm.at[pl.ds(s, SEG_CHUNK), :], row_buf.at[slot], sem.at[1, slot]))

    def start(b, c, slot):
        for cp in copies(b, c, slot):
            cp.start()

    def wait(b, c, slot):
        for cp in copies(b, c, slot):
            cp.wait()

    @pl.when(blk == 0)
    def _():
        start(blk, 0, 0)

    n_chunks = jnp.maximum((off_ref[blk + 1] - first_row(blk) + SEG_CHUNK - 1) // SEG_CHUNK, 1)
    acc_ref[...] = jnp.zeros_like(acc_ref)
    token = lax.broadcasted_iota(jnp.int32, (tm, SEG_CHUNK), 0) + blk * tm

    def chunk(c, carry):
        slot = c % 2
        wait(blk, c, slot)

        @pl.when(c + 1 < n_chunks)
        def _():
            start(blk, c + 1, 1 - slot)

        onehot = (tok_buf[slot] == token).astype(MXU_DTYPE)
        for dc in range(0, d, SEG_COLS):
            acc_ref[:, dc:dc + SEG_COLS] += jnp.dot(onehot, row_buf[slot, :, dc:dc + SEG_COLS],
                                                    preferred_element_type=F32)
        return carry

    lax.fori_loop(0, n_chunks, chunk, 0)

    @pl.when(blk + 1 < n_blk)
    def _():
        start(blk + 1, 0, 0)

    x2 = _layer_norm(alpha * x_ref[...] + acc_ref[...], g_ref[...], b_ref[...])
    xf_ref[...] = x2
    xb_ref[...] = x2.astype(xb_ref.dtype)


def _combine_ln(xf, offsets, tok_sorted, rows_sorted, g, b, alpha):
    n, d = xf.shape
    tm = SEG_TOKENS
    row = lambda: pl.BlockSpec((tm, d), lambda i, off: (i, 0))
    vec = lambda: pl.BlockSpec((1, d), lambda i, off: (0, 0))
    hbm = lambda: pl.BlockSpec(memory_space=pl.ANY)
    return pl.pallas_call(
        functools.partial(_combine_ln_kernel, alpha=alpha),
        grid_spec=pltpu.PrefetchScalarGridSpec(
            num_scalar_prefetch=1,
            grid=(n // tm,),
            in_specs=[row(), vec(), vec(), hbm(), hbm()],
            out_specs=[row(), row()],
            scratch_shapes=[pltpu.VMEM((2, 1, SEG_CHUNK), jnp.int32),
                            pltpu.VMEM((2, SEG_CHUNK, d), MXU_DTYPE),
                            pltpu.VMEM((tm, d), F32),
                            pltpu.SemaphoreType.DMA((2, 2))]),
        out_shape=[jax.ShapeDtypeStruct((n, d), F32), jax.ShapeDtypeStruct((n, d), MXU_DTYPE)],
        compiler_params=_params("arbitrary"),
        name="moe_combine_ln",
    )(offsets, xf, g.reshape(1, d), b.reshape(1, d), tok_sorted, rows_sorted)


def _moe_combine_ln(xf, xb, aff_t, w_gate, w_up, w_down, layer, g, b, alpha):
    n, d = xf.shape
    cap = max(1, CAPACITY_FACTOR * n // N_EXPERTS)
    gate, idx = lax.top_k(aff_t, cap)
    flat = idx.reshape(-1)
    xe = xb.at[flat].get(mode="promise_in_bounds")
    ye = _expert_ffn(xe, w_gate, w_up, w_down, layer, gate.reshape(-1, 1), cap)
    tok_sorted, order = lax.sort_key_val(flat, lax.iota(jnp.int32, flat.shape[0]))
    tok_sorted = jnp.pad(tok_sorted, (0, SEG_CHUNK), constant_values=-1)
    order = jnp.pad(order, (0, SEG_CHUNK))
    rows_sorted = ye.at[order].get(mode="promise_in_bounds")
    bounds = jnp.arange(n // SEG_TOKENS + 1, dtype=jnp.int32) * SEG_TOKENS
    offsets = jnp.searchsorted(tok_sorted[:flat.shape[0]], bounds, side="left").astype(jnp.int32)
    return _combine_ln(xf, offsets, tok_sorted.reshape(1, -1), rows_sorted, g, b, alpha)


def _trunk(x, weights, depth, alpha):
    batch, seq, d = x.shape
    n = batch * seq
    d_attn = (3 * d) // 8
    d_four = d // 4
    rope = _rope_tables(seq)
    dft = _dft_tables(seq, d_four)
    bias = _attn_bias()
    xf, xb = _emb_ln(x.reshape(n, d), weights["emb_ln_g"], weights["emb_ln_b"])
    big = ("w_in", "w_out", "w_router", "w_gate", "w_up", "w_down")
    for l in range(depth):
        w = {k: v[l] for k, v in weights.items() if not k.startswith("emb_") and k not in big}
        qkv, uf, uc = _in_proj(xb, weights["w_in"], l, rope, seq, d_attn, d_four)
        ya = _attention(qkv, bias, batch, seq, d_attn)
        yf = _fourier(uf, dft, batch, seq)
        yc = _conv_module(uc, w["conv_w"], w["conv_b"], w["conv_ln_g"], w["conv_ln_b"], batch, seq)
        x1f, x1b, aff_t = _out_proj(yf, ya, yc, xf, weights["w_out"], l, w["ln1_g"], w["ln1_b"],
                                    weights["w_router"], alpha)
        xf, xb = _moe_combine_ln(x1f, x1b, aff_t, weights["w_gate"], weights["w_up"], weights["w_down"], l,
                                 w["ln2_g"], w["ln2_b"], alpha)
    return xf.reshape(batch, seq, d)


def kernel(x_prompt, x_sample, emb_ln_g, emb_ln_b, w_in, conv_w, conv_b, conv_ln_g, conv_ln_b, w_out, ln1_g, ln1_b, w_router, w_gate, w_up, w_down, ln2_g, ln2_b):
    depth = w_in.shape[0]
    alpha = (2 * depth) ** 0.25
    w_router_pad = jnp.pad(w_router, ((0, 0), (0, 0), (0, LANES - w_router.shape[-1])))
    weights = dict(emb_ln_g=emb_ln_g, emb_ln_b=emb_ln_b, conv_w=conv_w, conv_b=conv_b, conv_ln_g=conv_ln_g,
                   conv_ln_b=conv_ln_b, ln1_g=ln1_g, ln1_b=ln1_b, ln2_g=ln2_g, ln2_b=ln2_b,
                   w_in=w_in.astype(MXU_DTYPE), w_out=w_out.astype(MXU_DTYPE),
                   w_router=w_router_pad.astype(MXU_DTYPE), w_gate=w_gate.astype(MXU_DTYPE),
                   w_up=w_up.astype(MXU_DTYPE), w_down=w_down.astype(MXU_DTYPE))
    y_prompt = _trunk(x_prompt, weights, depth, alpha)
    y_sample = _trunk(x_sample, weights, depth, alpha)
    return (y_prompt, y_sample)
```

```python
import functools
import math

import jax
import jax.numpy as jnp
from jax import lax
from jax.experimental import pallas as pl
from jax.experimental.pallas import tpu as pltpu

F32 = jnp.float32
MXU_DTYPE = jnp.bfloat16

HEAD_DIM = 64
FOURIER_GROUPS = 4
CONV_WIDTH = 31
CONV_PAD = CONV_WIDTH // 2
DILATIONS = (1, 4, 16)
ATTN_RADIUS = 64
ATTN_BLOCK = 64
ROPE_THETA = 500000.0
ROPE_DIM = HEAD_DIM // 4
N_EXPERTS = 16
CAPACITY_FACTOR = 2
LN_EPS = 1e-5
NEG_INF = -1e30
LOG2_E = math.log2(math.e)

LANES = 128
SUBLANES = 8
VMEM_LIMIT = 56 * 1024 * 1024

ROW_TILE = 512
ATTN_SUPER = ATTN_BLOCK * max(DILATIONS)
KEY_WINDOW = 3 * ATTN_BLOCK
KEY_PAD = 2 * LANES
SOFTMAX_ROWS = 32
MIX_ROWS = 64
ATTN_SKEW = 3
CONV_HALO = 16
CONV_ROWS = 64
OUT_SUB_ROWS = 256
FFN_ROW_TILE = 1024
FFN_HID_TILE = 512
FFN_SUB_ROWS = 512
SEG_TOKENS = 256
SEG_CHUNK = 256
SEG_ALIGN = 128
SEG_COLS = 512
SEG_BUFFERS = 4
DFT_P = 128
DFT_LANES = 8192
DFT_CHUNK = 1024
DFT_GROUP = 8


def _params(*sem):
    return pltpu.CompilerParams(dimension_semantics=sem, vmem_limit_bytes=VMEM_LIMIT)


def _layer_norm(x, g, b):
    mu = jnp.mean(x, axis=-1, keepdims=True)
    xc = x - mu
    var = jnp.mean(xc * xc, axis=-1, keepdims=True)
    return xc * lax.rsqrt(var + LN_EPS) * g + b


def _emb_ln_kernel(x_ref, g_ref, b_ref, xf_ref, xb_ref):
    y = _layer_norm(x_ref[...], g_ref[...], b_ref[...])
    xf_ref[...] = y
    xb_ref[...] = y.astype(xb_ref.dtype)


def _emb_ln(x, g, b):
    n, d = x.shape
    row = pl.BlockSpec((ROW_TILE, d), lambda i: (i, 0))
    vec = pl.BlockSpec((1, d), lambda i: (0, 0))
    return pl.pallas_call(
        _emb_ln_kernel,
        grid=(n // ROW_TILE,),
        in_specs=[row, vec, vec],
        out_specs=[row, row],
        out_shape=[jax.ShapeDtypeStruct((n, d), F32), jax.ShapeDtypeStruct((n, d), MXU_DTYPE)],
        compiler_params=_params("parallel"),
        name="emb_ln",
    )(x, g.reshape(1, d), b.reshape(1, d))


def _in_proj_kernel(x_ref, w_ref, cos_ref, sa_ref, sb_ref, qkv_ref, uf_ref, uc_ref, *, d_attn, d_four):
    x = x_ref[...]
    cos, sa, sb = cos_ref[...], sa_ref[...], sb_ref[...]
    half = ROPE_DIM // 2
    chunk = 2 * LANES
    d_qkv = 3 * d_attn
    for c in range(0, d_qkv, chunk):
        acc = jnp.dot(x, w_ref[:, c:c + chunk], preferred_element_type=F32)
        if c < 2 * d_attn:
            for h in range(0, chunk, LANES):
                a = acc[:, h:h + LANES]
                rot = a * cos + pltpu.roll(a, LANES - half, 1) * sa + pltpu.roll(a, half, 1) * sb
                qkv_ref[:, c + h:c + h + LANES] = rot
        else:
            qkv_ref[:, c:c + chunk] = acc
    for c in range(0, d_four, chunk):
        acc = jnp.dot(x, w_ref[:, d_qkv + c:d_qkv + c + chunk], preferred_element_type=F32)
        uf_ref[:, c:c + chunk] = acc.astype(uf_ref.dtype)
    d_conv2 = uc_ref.shape[1]
    for c in range(0, d_conv2, chunk):
        off = d_qkv + d_four + c
        uc_ref[:, c:c + chunk] = jnp.dot(x, w_ref[:, off:off + chunk], preferred_element_type=F32)


def _rope_tables(seq):
    half = ROPE_DIM // 2
    pos = jnp.arange(seq, dtype=F32)
    inv_freq = ROPE_THETA ** (-jnp.arange(0, ROPE_DIM, 2, dtype=F32) / ROPE_DIM)
    ang = pos[:, None] * inv_freq[None, :]
    cos, sin = jnp.cos(ang), jnp.sin(ang)
    rest = HEAD_DIM - ROPE_DIM
    ones = jnp.ones((seq, rest), F32)
    zeros = jnp.zeros((seq, rest), F32)
    zh = jnp.zeros((seq, half), F32)
    cos_h = jnp.concatenate([cos, cos, ones], axis=1)
    sa_h = jnp.concatenate([-sin, zh, zeros], axis=1)
    sb_h = jnp.concatenate([zh, sin, zeros], axis=1)
    reps = LANES // HEAD_DIM
    return tuple(jnp.tile(t, (1, reps)) for t in (cos_h, sa_h, sb_h))


def _in_proj(xb, w_in, layer, rope, seq, d_attn, d_four):
    n, d = xb.shape
    d_in = w_in.shape[2]
    d_conv2 = d_in - 3 * d_attn - d_four
    tiles_per_seq = seq // ROW_TILE
    row = lambda w: pl.BlockSpec((ROW_TILE, w), lambda i: (i, 0))
    tab = pl.BlockSpec((ROW_TILE, LANES), lambda i: (i % tiles_per_seq, 0))
    return pl.pallas_call(
        functools.partial(_in_proj_kernel, d_attn=d_attn, d_four=d_four),
        grid=(n // ROW_TILE,),
        in_specs=[row(d), pl.BlockSpec((None, d, d_in), lambda i: (layer, 0, 0), pipeline_mode=pl.Buffered(1)),
                  tab, tab, tab],
        out_specs=[row(3 * d_attn), row(d_four), row(d_conv2)],
        out_shape=[jax.ShapeDtypeStruct((n, 3 * d_attn), F32),
                   jax.ShapeDtypeStruct((n, d_four), MXU_DTYPE),
                   jax.ShapeDtypeStruct((n, d_conv2), F32)],
        compiler_params=_params("parallel"),
        name="in_proj",
    )(xb, w_in, *rope)


def _attn_kernel(q_ref, k_ref, v_ref, bias_ref, o_ref, s_scr, p_scr, m_scr, osc, msc, dsc, *, seq):
    lane = lax.broadcasted_iota(jnp.int32, (ATTN_BLOCK, LANES), 1)
    head_a = lane < HEAD_DIM
    scale = HEAD_DIM ** -0.5 * LOG2_E
    blocks_per_super = ATTN_SUPER // ATTN_BLOCK
    rows2 = 2 * ATTN_BLOCK

    def strided(start, size, d):
        return pl.ds(start, size) if d == 1 else pl.ds(start, size, stride=d)

    def super_block(sb, carry):
        for bi, d in enumerate(DILATIONS):
            sub_len = seq // d
            kw = min(KEY_WINDOW, sub_len)
            per_res = blocks_per_super // d

            def place(j, d=d, sub_len=sub_len, kw=kw, per_res=per_res):
                r = j % d
                bl = j // d
                i = sb * per_res + bl
                ks = jnp.clip(ATTN_BLOCK * (i - 1), 0, sub_len - kw)
                return r, bl, i, ks

            def scores(j, d=d, kw=kw, place=place):
                r, bl, i, ks = place(j)
                q = q_ref[strided(r + d * ATTN_BLOCK * i, ATTN_BLOCK, d), :] * scale
                kk = k_ref[strided(r + d * ks, kw, d), :]
                zero = jnp.zeros_like(q)
                q2 = jnp.concatenate([jnp.where(head_a, q, zero), jnp.where(head_a, zero, q)], axis=0)
                s_scr[j, :, :kw] = lax.dot_general(q2.astype(MXU_DTYPE), kk.astype(MXU_DTYPE),
                                                   (((1,), (1,)), ((), ())), preferred_element_type=F32)

            def softmax(j, kw=kw, place=place):
                r, bl, i, ks = place(j)
                case = (ATTN_BLOCK * i - ks) // ATTN_BLOCK
                for rc in range(0, rows2, SOFTMAX_ROWS):
                    s = jnp.minimum(s_scr[j, rc:rc + SOFTMAX_ROWS, :kw], bias_ref[case, rc:rc + SOFTMAX_ROWS, :kw])
                    m = jnp.max(s, axis=-1, keepdims=True)
                    p_scr[j, rc:rc + SOFTMAX_ROWS, :kw] = jnp.exp2(s - m).astype(p_scr.dtype)
                    m_scr[j, rc:rc + SOFTMAX_ROWS, :] = jnp.broadcast_to(m, (SOFTMAX_ROWS, LANES))

            def values(j, d=d, bi=bi, kw=kw, place=place):
                r, bl, i, ks = place(j)
                vv = v_ref[strided(r + d * ks, kw, d), :].astype(MXU_DTYPE)
                vext = jnp.concatenate([vv, jnp.ones_like(vv)], axis=1)
                od = jnp.dot(p_scr[j, :, :kw], vext, preferred_element_type=F32)
                dst = strided(r + d * ATTN_BLOCK * bl, ATTN_BLOCK, d)
                osc[bi, dst, :] = jnp.where(head_a, od[:ATTN_BLOCK, :LANES], od[ATTN_BLOCK:, :LANES])
                dsc[bi, dst, :] = jnp.where(head_a, od[:ATTN_BLOCK, LANES:], od[ATTN_BLOCK:, LANES:])
                msc[bi, dst, :] = jnp.where(head_a, m_scr[j, :ATTN_BLOCK, :], m_scr[j, ATTN_BLOCK:, :])

            for t in range(blocks_per_super + 2 * ATTN_SKEW):
                if t < blocks_per_super:
                    scores(t)
                if 0 <= t - ATTN_SKEW < blocks_per_super:
                    softmax(t - ATTN_SKEW)
                if 0 <= t - 2 * ATTN_SKEW < blocks_per_super:
                    values(t - 2 * ATTN_SKEW)

        def mix(rc, carry2):
            rows = pl.ds(pl.multiple_of(rc * MIX_ROWS, MIX_ROWS), MIX_ROWS)
            m0, m1, m2 = msc[0, rows, :], msc[1, rows, :], msc[2, rows, :]
            mx = jnp.maximum(jnp.maximum(m0, m1), m2)
            e0, e1, e2 = jnp.exp2(m0 - mx), jnp.exp2(m1 - mx), jnp.exp2(m2 - mx)
            num = e0 * osc[0, rows, :] + e1 * osc[1, rows, :] + e2 * osc[2, rows, :]
            den = e0 * dsc[0, rows, :] + e1 * dsc[1, rows, :] + e2 * dsc[2, rows, :]
            out_rows = pl.ds(pl.multiple_of(sb * ATTN_SUPER + rc * MIX_ROWS, MIX_ROWS), MIX_ROWS)
            o_ref[out_rows, :] = (num / den).astype(o_ref.dtype)
            return carry2

        lax.fori_loop(0, ATTN_SUPER // MIX_ROWS, mix, 0, unroll=2)
        return carry

    lax.fori_loop(0, seq // ATTN_SUPER, super_block, 0)


def _attn_bias():
    row = jnp.arange(2 * ATTN_BLOCK, dtype=jnp.int32)[:, None] % ATTN_BLOCK
    col = jnp.arange(KEY_PAD, dtype=jnp.int32)[None, :]
    off = jnp.arange(3, dtype=jnp.int32)[:, None, None] * ATTN_BLOCK
    ok = jnp.abs(row[None] - col[None] + off) <= ATTN_RADIUS
    return jnp.where(ok, jnp.inf, NEG_INF).astype(F32)


def _attention(qkv, bias, batch, seq, d_attn):
    pairs = d_attn // LANES
    qkv3 = qkv.reshape(batch, seq, 3 * d_attn)
    spec = lambda off: pl.BlockSpec((None, seq, LANES), lambda b, h: (b, 0, off + h))
    blocks_per_super = ATTN_SUPER // ATTN_BLOCK
    out = pl.pallas_call(
        functools.partial(_attn_kernel, seq=seq),
        grid=(batch, pairs),
        in_specs=[spec(0), spec(pairs), spec(2 * pairs),
                  pl.BlockSpec(bias.shape, lambda b, h: (0, 0, 0))],
        out_specs=pl.BlockSpec((None, seq, LANES), lambda b, h: (b, 0, h)),
        out_shape=jax.ShapeDtypeStruct((batch, seq, d_attn), MXU_DTYPE),
        scratch_shapes=[pltpu.VMEM((blocks_per_super, 2 * ATTN_BLOCK, KEY_PAD), F32),
                        pltpu.VMEM((blocks_per_super, 2 * ATTN_BLOCK, KEY_PAD), MXU_DTYPE),
                        pltpu.VMEM((blocks_per_super, 2 * ATTN_BLOCK, LANES), F32),
                        pltpu.VMEM((len(DILATIONS), ATTN_SUPER, LANES), F32),
                        pltpu.VMEM((len(DILATIONS), ATTN_SUPER, LANES), F32),
                        pltpu.VMEM((len(DILATIONS), ATTN_SUPER, LANES), F32)],
        compiler_params=_params("parallel", "parallel"),
        name="dilated_attention",
    )(qkv3, qkv3, qkv3, bias)
    return out.reshape(batch * seq, d_attn)


def _dft_stage1_kernel(w_ref, x_ref, tc_ref, ts_ref, re_ref, nim_ref, *, q):
    w = w_ref[...]
    for c in range(0, x_ref.shape[1], DFT_CHUNK):
        r = jnp.dot(w, x_ref[:, c:c + DFT_CHUNK], preferred_element_type=F32)
        cr, sr = r[:q], r[q:]
        tc, ts = tc_ref[:, c:c + DFT_CHUNK], ts_ref[:, c:c + DFT_CHUNK]
        re_ref[:, c:c + DFT_CHUNK] = (cr * tc - sr * ts).astype(re_ref.dtype)
        nim_ref[:, c:c + DFT_CHUNK] = (cr * ts + sr * tc).astype(nim_ref.dtype)


def _dft_stage2_kernel(re_ref, nim_ref, w2_ref, chan_ref, o_ref, *, p):
    for r in range(0, re_ref.shape[0], p):
        t = jnp.concatenate([re_ref[r:r + p, :], nim_ref[r:r + p, :]], axis=0)
        z = jnp.dot(w2_ref[...], t, preferred_element_type=F32)
        zz = jnp.concatenate([z[:p], z[p:]], axis=1).astype(MXU_DTYPE)
        o_ref[r:r + p, :] = jnp.dot(zz, chan_ref[...], preferred_element_type=F32).astype(o_ref.dtype)


def _dft_tables(seq, d_four):
    p = DFT_P
    q = seq // p

    def cos_sin(n_rows, n_cols, period):
        i = jnp.arange(n_rows, dtype=jnp.int32)[:, None] * jnp.arange(n_cols, dtype=jnp.int32)[None, :]
        ang = (i % period).astype(F32) * (2.0 * math.pi / period)
        return jnp.cos(ang), jnp.sin(ang)

    cq, sq = cos_sin(q, q, q)
    w1 = (jnp.concatenate([cq, sq], axis=0) * q ** -0.5).astype(MXU_DTYPE)
    tc, ts = cos_sin(q, p, seq)
    tc, ts = jnp.repeat(tc, d_four, axis=1), jnp.repeat(ts, d_four, axis=1)
    cp, sp = cos_sin(p, p, p)
    w2 = (jnp.block([[cp, -sp], [sp, cp]]) * p ** -0.5).astype(MXU_DTYPE)
    gw = d_four // FOURIER_GROUPS
    cc, sc = cos_sin(gw, gw, gw)
    eye = jnp.eye(FOURIER_GROUPS, dtype=F32)
    chan = (jnp.concatenate([jnp.kron(eye, cc), -jnp.kron(eye, sc)], axis=0) * gw ** -0.5).astype(MXU_DTYPE)
    return w1, tc, ts, w2, chan


def _fourier(uf, dft, batch, seq):
    w1, tc, ts, w2, chan = dft
    d_four = uf.shape[1]
    p = DFT_P
    q = seq // p
    flat = p * d_four
    x3 = uf.reshape(batch, q, flat)
    xspec = pl.BlockSpec((None, q, DFT_LANES), lambda j, b: (b, 0, j))
    tspec = pl.BlockSpec((q, DFT_LANES), lambda j, b: (0, j))
    t_re, t_nim = pl.pallas_call(
        functools.partial(_dft_stage1_kernel, q=q),
        grid=(flat // DFT_LANES, batch),
        in_specs=[pl.BlockSpec((2 * q, q), lambda j, b: (0, 0)), xspec, tspec, tspec],
        out_specs=[xspec, xspec],
        out_shape=[jax.ShapeDtypeStruct((batch, q, flat), MXU_DTYPE)] * 2,
        compiler_params=_params("parallel", "arbitrary"),
        name="fourier_stage1",
    )(w1, x3, tc, ts)
    rows = DFT_GROUP * p
    rspec = pl.BlockSpec((None, rows, d_four), lambda b, i: (b, i, 0))
    y = pl.pallas_call(
        functools.partial(_dft_stage2_kernel, p=p),
        grid=(batch, q // DFT_GROUP),
        in_specs=[rspec, rspec,
                  pl.BlockSpec((2 * p, 2 * p), lambda b, i: (0, 0)),
                  pl.BlockSpec((2 * d_four, d_four), lambda b, i: (0, 0))],
        out_specs=rspec,
        out_shape=jax.ShapeDtypeStruct((batch, seq, d_four), MXU_DTYPE),
        compiler_params=_params("parallel", "parallel"),
        name="fourier_stage2",
    )(t_re.reshape(batch, seq, d_four), t_nim.reshape(batch, seq, d_four), w2, chan)
    y = y.reshape(batch, q, p, d_four).transpose(0, 2, 1, 3)
    return y.reshape(batch * seq, d_four)


def _conv_kernel(cur_ref, prev_ref, next_ref, w_ref, b_ref, g_ref, beta_ref, o_ref, hp_ref, acc_ref, *, d_conv):
    i = pl.program_id(1)
    last = pl.num_programs(1) - 1
    tile = cur_ref.shape[0]

    def glu(ref):
        return ref[:, :d_conv] * jax.nn.sigmoid(ref[:, d_conv:])

    hp_ref[pl.ds(0, CONV_HALO), :] = jnp.where(i > 0, glu(prev_ref), 0.0)
    hp_ref[pl.ds(CONV_HALO, tile), :] = glu(cur_ref)
    hp_ref[pl.ds(CONV_HALO + tile, CONV_HALO), :] = jnp.where(i < last, glu(next_ref), 0.0)

    first_tap = CONV_HALO - CONV_PAD
    for c in range(0, d_conv, LANES):
        taps = [w_ref[pl.ds(j, 1), c:c + LANES] for j in range(CONV_WIDTH)]

        def rows(rc, carry, c=c, taps=taps):
            base = pl.multiple_of(rc * CONV_ROWS, CONV_ROWS)
            win = hp_ref[pl.ds(base, CONV_ROWS + 2 * CONV_HALO), c:c + LANES]
            acc = jnp.zeros((CONV_ROWS, LANES), F32)
            for sh in range(SUBLANES):
                part = None
                for j in range(sh, CONV_WIDTH, SUBLANES):
                    term = win[j - sh:j - sh + CONV_ROWS + SUBLANES] * taps[j]
                    part = term if part is None else part + term
                acc = acc + part[first_tap + sh:first_tap + sh + CONV_ROWS]
            acc_ref[pl.ds(base, CONV_ROWS), c:c + LANES] = acc
            return carry

        lax.fori_loop(0, tile // CONV_ROWS, rows, 0)

    h = _layer_norm(acc_ref[...] + b_ref[...], g_ref[...], beta_ref[...])
    o_ref[...] = (h * jax.nn.sigmoid(h)).astype(o_ref.dtype)


def _conv_module(uc, conv_w, conv_b, ln_g, ln_b, batch, seq):
    d_conv = uc.shape[1] // 2
    tile = ROW_TILE
    halo_per_tile = tile // CONV_HALO
    n_halo = seq // CONV_HALO
    u3 = uc.reshape(batch, seq, 2 * d_conv)
    vec = pl.BlockSpec((1, d_conv), lambda b, i: (0, 0))
    out = pl.pallas_call(
        functools.partial(_conv_kernel, d_conv=d_conv),
        grid=(batch, seq // tile),
        in_specs=[pl.BlockSpec((None, tile, 2 * d_conv), lambda b, i: (b, i, 0)),
                  pl.BlockSpec((None, CONV_HALO, 2 * d_conv),
                               lambda b, i: (b, jnp.maximum(i * halo_per_tile - 1, 0), 0)),
                  pl.BlockSpec((None, CONV_HALO, 2 * d_conv),
                               lambda b, i: (b, jnp.minimum((i + 1) * halo_per_tile, n_halo - 1), 0)),
                  pl.BlockSpec((CONV_WIDTH, d_conv), lambda b, i: (0, 0)),
                  vec, vec, vec],
        out_specs=pl.BlockSpec((None, tile, d_conv), lambda b, i: (b, i, 0)),
        out_shape=jax.ShapeDtypeStruct((batch, seq, d_conv), MXU_DTYPE),
        scratch_shapes=[pltpu.VMEM((tile + 2 * CONV_HALO, d_conv), F32), pltpu.VMEM((tile, d_conv), F32)],
        compiler_params=_params("parallel", "parallel"),
        name="conv_module",
    )(u3, u3, u3, conv_w, conv_b.reshape(1, d_conv), ln_g.reshape(1, d_conv), ln_b.reshape(1, d_conv))
    return out.reshape(batch * seq, d_conv)


def _out_proj_kernel(yf_ref, ya_ref, yc_ref, x_ref, w_ref, g_ref, b_ref, wr_ref, xf_ref, xb_ref, aff_ref, *, alpha):
    d_four, d_attn = yf_ref.shape[1], ya_ref.shape[1]
    for r in range(0, x_ref.shape[0], OUT_SUB_ROWS):
        rows = slice(r, r + OUT_SUB_ROWS)
        mix = jnp.dot(yf_ref[rows, :], w_ref[:d_four, :], preferred_element_type=F32)
        mix += jnp.dot(ya_ref[rows, :], w_ref[d_four:d_four + d_attn, :], preferred_element_type=F32)
        mix += jnp.dot(yc_ref[rows, :], w_ref[d_four + d_attn:, :], preferred_element_type=F32)
        x1 = _layer_norm(alpha * x_ref[rows, :] + mix, g_ref[...], b_ref[...])
        xf_ref[rows, :] = x1
        xb = x1.astype(xb_ref.dtype)
        xb_ref[rows, :] = xb
        logits = jnp.dot(xb, wr_ref[...], preferred_element_type=F32)
        lt = logits.T[:N_EXPERTS, :]
        e = jnp.exp(lt - jnp.max(lt, axis=0, keepdims=True))
        aff_ref[:, rows] = e / jnp.sum(e, axis=0, keepdims=True)


def _out_proj(yf, ya, yc, xf, w_out, layer, g, b, w_router_pad, alpha):
    n, d = xf.shape
    row = lambda w: pl.BlockSpec((ROW_TILE, w), lambda i: (i, 0))
    vec = pl.BlockSpec((1, d), lambda i: (0, 0))
    return pl.pallas_call(
        functools.partial(_out_proj_kernel, alpha=alpha),
        grid=(n // ROW_TILE,),
        in_specs=[row(yf.shape[1]), row(ya.shape[1]), row(yc.shape[1]), row(d),
                  pl.BlockSpec((None,) + w_out.shape[1:], lambda i: (layer, 0, 0), pipeline_mode=pl.Buffered(1)),
                  vec, vec,
                  pl.BlockSpec((None,) + w_router_pad.shape[1:], lambda i: (layer, 0, 0),
                               pipeline_mode=pl.Buffered(1))],
        out_specs=[row(d), row(d), pl.BlockSpec((N_EXPERTS, ROW_TILE), lambda i: (0, i))],
        out_shape=[jax.ShapeDtypeStruct((n, d), F32), jax.ShapeDtypeStruct((n, d), MXU_DTYPE),
                   jax.ShapeDtypeStruct((N_EXPERTS, n), F32)],
        compiler_params=_params("parallel"),
        name="out_proj_ln_router",
    )(yf, ya, yc, xf, w_out, g.reshape(1, d), b.reshape(1, d), w_router_pad)


def _ffn_kernel(x_ref, wg_ref, wu_ref, wd_ref, gate_ref, o_ref, acc_ref):
    f = pl.program_id(2)

    @pl.when(f == 0)
    def _():
        acc_ref[...] = jnp.zeros_like(acc_ref)

    sub = min(FFN_SUB_ROWS, x_ref.shape[0])
    for r in range(0, x_ref.shape[0], sub):
        x = x_ref[r:r + sub, :]
        g = jnp.dot(x, wg_ref[...], preferred_element_type=F32)
        u = jnp.dot(x, wu_ref[...], preferred_element_type=F32)
        hid = (g * jax.nn.sigmoid(g) * u).astype(MXU_DTYPE)
        acc_ref[r:r + sub, :] += jnp.dot(hid, wd_ref[...], preferred_element_type=F32)

    @pl.when(f == pl.num_programs(2) - 1)
    def _():
        o_ref[...] = (acc_ref[...] * gate_ref[...]).astype(o_ref.dtype)


def _expert_ffn(xe, w_gate, w_up, w_down, layer, gate, cap):
    rows, d = xe.shape
    _, n_exp, _, d_hid = w_gate.shape
    tm = min(FFN_ROW_TILE, cap)
    tf = FFN_HID_TILE
    tiles = cap // tm
    return pl.pallas_call(
        _ffn_kernel,
        grid=(n_exp, tiles, d_hid // tf),
        in_specs=[pl.BlockSpec((tm, d), lambda e, i, f: (e * tiles + i, 0)),
                  pl.BlockSpec((None, None, d, tf), lambda e, i, f: (layer, e, 0, f)),
                  pl.BlockSpec((None, None, d, tf), lambda e, i, f: (layer, e, 0, f)),
                  pl.BlockSpec((None, None, tf, d), lambda e, i, f: (layer, e, f, 0)),
                  pl.BlockSpec((tm, 1), lambda e, i, f: (e * tiles + i, 0))],
        out_specs=pl.BlockSpec((tm, d), lambda e, i, f: (e * tiles + i, 0)),
        out_shape=jax.ShapeDtypeStruct((rows, d), MXU_DTYPE),
        scratch_shapes=[pltpu.VMEM((tm, d), F32)],
        compiler_params=_params("parallel", "parallel", "arbitrary"),
        name="expert_ffn",
    )(xe, w_gate, w_up, w_down, gate)


def _combine_ln_kernel(off_ref, x_ref, g_ref, b_ref, tok_hbm, rows_hbm, xf_ref, xb_ref,
                       tok_buf, row_buf, acc_ref, sem, *, alpha):
    blk = pl.program_id(0)
    n_blk = pl.num_programs(0)
    tm, d = x_ref.shape

    def first_row(b):
        return (off_ref[b] // SEG_ALIGN) * SEG_ALIGN

    def chunks_of(b):
        return jnp.maximum((off_ref[b + 1] - first_row(b) + SEG_CHUNK - 1) // SEG_CHUNK, 1)

    def copies(b, c, slot):
        s = pl.multiple_of(first_row(b) + c * SEG_CHUNK, SEG_ALIGN)
        return (pltpu.make_async_copy(tok_hbm.at[:, pl.ds(s, SEG_CHUNK)], tok_buf.at[slot], sem.at[0, slot]),
                pltpu.make_async_copy(rows_hbm.at[pl.ds(s, SEG_CHUNK), :], row_buf.at[slot], sem.at[1, slot]))

    def start(b, c, slot):
        for cp in copies(b, c, slot):
            cp.start()

    def wait(b, c, slot):
        for cp in copies(b, c, slot):
            cp.wait()

    def start_leading(b):
        n = chunks_of(b)
        for c in range(SEG_BUFFERS):
            @pl.when(c < n)
            def _():
                start(b, c, c)

    @pl.when(blk == 0)
    def _():
        start_leading(blk)

    n_chunks = chunks_of(blk)
    acc_ref[...] = jnp.zeros_like(acc_ref)
    token = lax.broadcasted_iota(jnp.int32, (tm, SEG_CHUNK), 0) + blk * tm

    def chunk(c, carry):
        slot = c % SEG_BUFFERS
        wait(blk, c, slot)
        onehot = (tok_buf[slot] == token).astype(MXU_DTYPE)
        for dc in range(0, d, SEG_COLS):
            acc_ref[:, dc:dc + SEG_COLS] += jnp.dot(onehot, row_buf[slot, :, dc:dc + SEG_COLS],
                                                    preferred_element_type=F32)

        @pl.when(c + SEG_BUFFERS < n_chunks)
        def _():
            start(blk, c + SEG_BUFFERS, slot)

        return carry

    lax.fori_loop(0, n_chunks, chunk, 0)

    @pl.when(blk + 1 < n_blk)
    def _():
        start_leading(blk + 1)

    x2 = _layer_norm(alpha * x_ref[...] + acc_ref[...], g_ref[...], b_ref[...])
    xf_ref[...] = x2
    xb_ref[...] = x2.astype(xb_ref.dtype)


def _combine_ln(xf, offsets, tok_sorted, rows_sorted, g, b, alpha):
    n, d = xf.shape
    tm = SEG_TOKENS
    row = lambda: pl.BlockSpec((tm, d), lambda i, off: (i, 0))
    vec = lambda: pl.BlockSpec((1, d), lambda i, off: (0, 0))
    hbm = lambda: pl.BlockSpec(memory_space=pl.ANY)
    return pl.pallas_call(
        functools.partial(_combine_ln_kernel, alpha=alpha),
        grid_spec=pltpu.PrefetchScalarGridSpec(
            num_scalar_prefetch=1,
            grid=(n // tm,),
            in_specs=[row(), vec(), vec(), hbm(), hbm()],
            out_specs=[row(), row()],
            scratch_shapes=[pltpu.VMEM((SEG_BUFFERS, 1, SEG_CHUNK), jnp.int32),
                            pltpu.VMEM((SEG_BUFFERS, SEG_CHUNK, d), MXU_DTYPE),
                            pltpu.VMEM((tm, d), F32),
                            pltpu.SemaphoreType.DMA((2, SEG_BUFFERS))]),
        out_shape=[jax.ShapeDtypeStruct((n, d), F32), jax.ShapeDtypeStruct((n, d), MXU_DTYPE)],
        compiler_params=_params("arbitrary"),
        name="moe_combine_ln",
    )(offsets, xf, g.reshape(1, d), b.reshape(1, d), tok_sorted, rows_sorted)


def _moe_combine_ln(xf, xb, aff_t, w_gate, w_up, w_down, layer, g, b, alpha):
    n, d = xf.shape
    cap = max(1, CAPACITY_FACTOR * n // N_EXPERTS)
    gate, idx = lax.top_k(aff_t, cap)
    flat = idx.reshape(-1)
    xe = xb.at[flat].get(mode="promise_in_bounds")
    ye = _expert_ffn(xe, w_gate, w_up, w_down, layer, gate.reshape(-1, 1), cap)
    tok_sorted, order = lax.sort_key_val(flat, lax.iota(jnp.int32, flat.shape[0]))
    tok_sorted = jnp.pad(tok_sorted, (0, SEG_CHUNK), constant_values=-1)
    order = jnp.pad(order, (0, SEG_CHUNK))
    rows_sorted = ye.at[order].get(mode="promise_in_bounds")
    bounds = jnp.arange(n // SEG_TOKENS + 1, dtype=jnp.int32) * SEG_TOKENS
    offsets = jnp.searchsorted(tok_sorted[:flat.shape[0]], bounds, side="left").astype(jnp.int32)
    return _combine_ln(xf, offsets, tok_sorted.reshape(1, -1), rows_sorted, g, b, alpha)


def _trunk(x, weights, depth, alpha):
    batch, seq, d = x.shape
    n = batch * seq
    d_attn = (3 * d) // 8
    d_four = d // 4
    rope = _rope_tables(seq)
    dft = _dft_tables(seq, d_four)
    bias = _attn_bias()
    xf, xb = _emb_ln(x.reshape(n, d), weights["emb_ln_g"], weights["emb_ln_b"])
    big = ("w_in", "w_out", "w_router", "w_gate", "w_up", "w_down")
    for l in range(depth):
        w = {k: v[l] for k, v in weights.items() if not k.startswith("emb_") and k not in big}
        qkv, uf, uc = _in_proj(xb, weights["w_in"], l, rope, seq, d_attn, d_four)
        ya = _attention(qkv, bias, batch, seq, d_attn)
        yf = _fourier(uf, dft, batch, seq)
        yc = _conv_module(uc, w["conv_w"], w["conv_b"], w["conv_ln_g"], w["conv_ln_b"], batch, seq)
        x1f, x1b, aff_t = _out_proj(yf, ya, yc, xf, weights["w_out"], l, w["ln1_g"], w["ln1_b"],
                                    weights["w_router"], alpha)
        xf, xb = _moe_combine_ln(x1f, x1b, aff_t, weights["w_gate"], weights["w_up"], weights["w_down"], l,
                                 w["ln2_g"], w["ln2_b"], alpha)
    return xf.reshape(batch, seq, d)


def kernel(x_prompt, x_sample, emb_ln_g, emb_ln_b, w_in, conv_w, conv_b, conv_ln_g, conv_ln_b, w_out, ln1_g, ln1_b, w_router, w_gate, w_up, w_down, ln2_g, ln2_b):
    depth = w_in.shape[0]
    alpha = (2 * depth) ** 0.25
    w_router_pad = jnp.pad(w_router, ((0, 0), (0, 0), (0, LANES - w_router.shape[-1])))
    weights = dict(emb_ln_g=emb_ln_g, emb_ln_b=emb_ln_b, conv_w=conv_w, conv_b=conv_b, conv_ln_g=conv_ln_g,
                   conv_ln_b=conv_ln_b, ln1_g=ln1_g, ln1_b=ln1_b, ln2_g=ln2_g, ln2_b=ln2_b,
                   w_in=w_in.astype(MXU_DTYPE), w_out=w_out.astype(MXU_DTYPE),
                   w_router=w_router_pad.astype(MXU_DTYPE), w_gate=w_gate.astype(MXU_DTYPE),
                   w_up=w_up.astype(MXU_DTYPE), w_down=w_down.astype(MXU_DTYPE))
    y_prompt = _trunk(x_prompt, weights, depth, alpha)
    y_sample = _trunk(x_sample, weights, depth, alpha)
    return (y_prompt, y_sample)
```

```python
import functools
import math

import jax
import jax.numpy as jnp
from jax import lax
from jax.experimental import pallas as pl
from jax.experimental.pallas import tpu as pltpu

F32 = jnp.float32
MXU_DTYPE = jnp.bfloat16

HEAD_DIM = 64
FOURIER_GROUPS = 4
CONV_WIDTH = 31
CONV_PAD = CONV_WIDTH // 2
DILATIONS = (1, 4, 16)
ATTN_RADIUS = 64
ATTN_BLOCK = 64
ROPE_THETA = 500000.0
ROPE_DIM = HEAD_DIM // 4
N_EXPERTS = 16
CAPACITY_FACTOR = 2
LN_EPS = 1e-5
NEG_INF = -1e30
LOG2_E = math.log2(math.e)

LANES = 128
SUBLANES = 8
VMEM_LIMIT = 56 * 1024 * 1024

ROW_TILE = 512
ATTN_SUPER = ATTN_BLOCK * max(DILATIONS)
KEY_WINDOW = 3 * ATTN_BLOCK
KEY_PAD = 2 * LANES
SOFTMAX_ROWS = 32
MIX_ROWS = 64
ATTN_SKEW = 3
CONV_HALO = 16
CONV_ROWS = 64
OUT_SUB_ROWS = 256
FFN_ROW_TILE = 1024
FFN_HID_TILE = 512
FFN_SUB_ROWS = 512
SEG_TOKENS = 256
SEG_CHUNK = 256
SEG_ALIGN = 128
SEG_COLS = 512
SEG_BUFFERS = 4
DFT_P = 128
DFT_LANES = 8192
DFT_CHUNK = 1024
DFT_GROUP = 8


def _params(*sem):
    return pltpu.CompilerParams(dimension_semantics=sem, vmem_limit_bytes=VMEM_LIMIT)


def _layer_norm(x, g, b):
    mu = jnp.mean(x, axis=-1, keepdims=True)
    xc = x - mu
    var = jnp.mean(xc * xc, axis=-1, keepdims=True)
    return xc * lax.rsqrt(var + LN_EPS) * g + b


def _emb_ln_kernel(x_ref, g_ref, b_ref, xf_ref, xb_ref):
    y = _layer_norm(x_ref[...], g_ref[...], b_ref[...])
    xf_ref[...] = y
    xb_ref[...] = y.astype(xb_ref.dtype)


def _emb_ln(x, g, b):
    n, d = x.shape
    row = pl.BlockSpec((ROW_TILE, d), lambda i: (i, 0))
    vec = pl.BlockSpec((1, d), lambda i: (0, 0))
    return pl.pallas_call(
        _emb_ln_kernel,
        grid=(n // ROW_TILE,),
        in_specs=[row, vec, vec],
        out_specs=[row, row],
        out_shape=[jax.ShapeDtypeStruct((n, d), F32), jax.ShapeDtypeStruct((n, d), MXU_DTYPE)],
        compiler_params=_params("parallel"),
        name="emb_ln",
    )(x, g.reshape(1, d), b.reshape(1, d))


def _in_proj_kernel(x_ref, w_ref, cos_ref, sa_ref, sb_ref, qkv_ref, uf_ref, uc_ref, *, d_attn, d_four):
    x = x_ref[...]
    cos, sa, sb = cos_ref[...], sa_ref[...], sb_ref[...]
    half = ROPE_DIM // 2
    chunk = 2 * LANES
    d_qkv = 3 * d_attn
    for c in range(0, d_qkv, chunk):
        acc = jnp.dot(x, w_ref[:, c:c + chunk], preferred_element_type=F32)
        if c < 2 * d_attn:
            for h in range(0, chunk, LANES):
                a = acc[:, h:h + LANES]
                rot = a * cos + pltpu.roll(a, LANES - half, 1) * sa + pltpu.roll(a, half, 1) * sb
                qkv_ref[:, c + h:c + h + LANES] = rot
        else:
            qkv_ref[:, c:c + chunk] = acc
    for c in range(0, d_four, chunk):
        acc = jnp.dot(x, w_ref[:, d_qkv + c:d_qkv + c + chunk], preferred_element_type=F32)
        uf_ref[:, c:c + chunk] = acc.astype(uf_ref.dtype)
    d_conv2 = uc_ref.shape[1]
    for c in range(0, d_conv2, chunk):
        off = d_qkv + d_four + c
        uc_ref[:, c:c + chunk] = jnp.dot(x, w_ref[:, off:off + chunk], preferred_element_type=F32)


def _rope_tables(seq):
    half = ROPE_DIM // 2
    pos = jnp.arange(seq, dtype=F32)
    inv_freq = ROPE_THETA ** (-jnp.arange(0, ROPE_DIM, 2, dtype=F32) / ROPE_DIM)
    ang = pos[:, None] * inv_freq[None, :]
    cos, sin = jnp.cos(ang), jnp.sin(ang)
    rest = HEAD_DIM - ROPE_DIM
    ones = jnp.ones((seq, rest), F32)
    zeros = jnp.zeros((seq, rest), F32)
    zh = jnp.zeros((seq, half), F32)
    cos_h = jnp.concatenate([cos, cos, ones], axis=1)
    sa_h = jnp.concatenate([-sin, zh, zeros], axis=1)
    sb_h = jnp.concatenate([zh, sin, zeros], axis=1)
    reps = LANES // HEAD_DIM
    return tuple(jnp.tile(t, (1, reps)) for t in (cos_h, sa_h, sb_h))


def _in_proj(xb, w_in, layer, rope, seq, d_attn, d_four):
    n, d = xb.shape
    d_in = w_in.shape[2]
    d_conv2 = d_in - 3 * d_attn - d_four
    tiles_per_seq = seq // ROW_TILE
    row = lambda w: pl.BlockSpec((ROW_TILE, w), lambda i: (i, 0))
    tab = pl.BlockSpec((ROW_TILE, LANES), lambda i: (i % tiles_per_seq, 0))
    return pl.pallas_call(
        functools.partial(_in_proj_kernel, d_attn=d_attn, d_four=d_four),
        grid=(n // ROW_TILE,),
        in_specs=[row(d), pl.BlockSpec((None, d, d_in), lambda i: (layer, 0, 0), pipeline_mode=pl.Buffered(1)),
                  tab, tab, tab],
        out_specs=[row(3 * d_attn), row(d_four), row(d_conv2)],
        out_shape=[jax.ShapeDtypeStruct((n, 3 * d_attn), F32),
                   jax.ShapeDtypeStruct((n, d_four), MXU_DTYPE),
                   jax.ShapeDtypeStruct((n, d_conv2), F32)],
        compiler_params=_params("parallel"),
        name="in_proj",
    )(xb, w_in, *rope)


def _attn_kernel(q_ref, k_ref, v_ref, bias_ref, o_ref, s_scr, p_scr, m_scr, osc, msc, dsc, *, seq):
    lane = lax.broadcasted_iota(jnp.int32, (ATTN_BLOCK, LANES), 1)
    head_a = lane < HEAD_DIM
    scale = HEAD_DIM ** -0.5 * LOG2_E
    blocks_per_super = ATTN_SUPER // ATTN_BLOCK
    rows2 = 2 * ATTN_BLOCK

    def strided(start, size, d):
        return pl.ds(start, size) if d == 1 else pl.ds(start, size, stride=d)

    def super_block(sb, carry):
        for bi, d in enumerate(DILATIONS):
            sub_len = seq // d
            kw = min(KEY_WINDOW, sub_len)
            per_res = blocks_per_super // d

            def place(j, d=d, sub_len=sub_len, kw=kw, per_res=per_res):
                r = j % d
                bl = j // d
                i = sb * per_res + bl
                ks = jnp.clip(ATTN_BLOCK * (i - 1), 0, sub_len - kw)
                return r, bl, i, ks

            def scores(j, d=d, kw=kw, place=place):
                r, bl, i, ks = place(j)
                q = q_ref[strided(r + d * ATTN_BLOCK * i, ATTN_BLOCK, d), :] * scale
                kk = k_ref[strided(r + d * ks, kw, d), :]
                zero = jnp.zeros_like(q)
                q2 = jnp.concatenate([jnp.where(head_a, q, zero), jnp.where(head_a, zero, q)], axis=0)
                s_scr[j, :, :kw] = lax.dot_general(q2.astype(MXU_DTYPE), kk.astype(MXU_DTYPE),
                                                   (((1,), (1,)), ((), ())), preferred_element_type=F32)

            def softmax(j, kw=kw, place=place):
                r, bl, i, ks = place(j)
                case = (ATTN_BLOCK * i - ks) // ATTN_BLOCK
                for rc in range(0, rows2, SOFTMAX_ROWS):
                    s = jnp.minimum(s_scr[j, rc:rc + SOFTMAX_ROWS, :kw], bias_ref[case, rc:rc + SOFTMAX_ROWS, :kw])
                    m = jnp.max(s, axis=-1, keepdims=True)
                    p_scr[j, rc:rc + SOFTMAX_ROWS, :kw] = jnp.exp2(s - m).astype(p_scr.dtype)
                    m_scr[j, rc:rc + SOFTMAX_ROWS, :] = jnp.broadcast_to(m, (SOFTMAX_ROWS, LANES))

            def values(j, d=d, bi=bi, kw=kw, place=place):
                r, bl, i, ks = place(j)
                vv = v_ref[strided(r + d * ks, kw, d), :].astype(MXU_DTYPE)
                vext = jnp.concatenate([vv, jnp.ones_like(vv)], axis=1)
                od = jnp.dot(p_scr[j, :, :kw], vext, preferred_element_type=F32)
                dst = strided(r + d * ATTN_BLOCK * bl, ATTN_BLOCK, d)
                osc[bi, dst, :] = jnp.where(head_a, od[:ATTN_BLOCK, :LANES], od[ATTN_BLOCK:, :LANES])
                dsc[bi, dst, :] = jnp.where(head_a, od[:ATTN_BLOCK, LANES:], od[ATTN_BLOCK:, LANES:])
                msc[bi, dst, :] = jnp.where(head_a, m_scr[j, :ATTN_BLOCK, :], m_scr[j, ATTN_BLOCK:, :])

            for t in range(blocks_per_super + 2 * ATTN_SKEW):
                if t < blocks_per_super:
                    scores(t)
                if 0 <= t - ATTN_SKEW < blocks_per_super:
                    softmax(t - ATTN_SKEW)
                if 0 <= t - 2 * ATTN_SKEW < blocks_per_super:
                    values(t - 2 * ATTN_SKEW)

        def mix(rc, carry2):
            rows = pl.ds(pl.multiple_of(rc * MIX_ROWS, MIX_ROWS), MIX_ROWS)
            m0, m1, m2 = msc[0, rows, :], msc[1, rows, :], msc[2, rows, :]
            mx = jnp.maximum(jnp.maximum(m0, m1), m2)
            e0, e1, e2 = jnp.exp2(m0 - mx), jnp.exp2(m1 - mx), jnp.exp2(m2 - mx)
            num = e0 * osc[0, rows, :] + e1 * osc[1, rows, :] + e2 * osc[2, rows, :]
            den = e0 * dsc[0, rows, :] + e1 * dsc[1, rows, :] + e2 * dsc[2, rows, :]
            out_rows = pl.ds(pl.multiple_of(sb * ATTN_SUPER + rc * MIX_ROWS, MIX_ROWS), MIX_ROWS)
            o_ref[out_rows, :] = (num / den).astype(o_ref.dtype)
            return carry2

        lax.fori_loop(0, ATTN_SUPER // MIX_ROWS, mix, 0, unroll=2)
        return carry

    lax.fori_loop(0, seq // ATTN_SUPER, super_block, 0)


def _attn_bias():
    row = jnp.arange(2 * ATTN_BLOCK, dtype=jnp.int32)[:, None] % ATTN_BLOCK
    col = jnp.arange(KEY_PAD, dtype=jnp.int32)[None, :]
    off = jnp.arange(3, dtype=jnp.int32)[:, None, None] * ATTN_BLOCK
    ok = jnp.abs(row[None] - col[None] + off) <= ATTN_RADIUS
    return jnp.where(ok, jnp.inf, NEG_INF).astype(F32)


def _attention(qkv, bias, batch, seq, d_attn):
    pairs = d_attn // LANES
    qkv3 = qkv.reshape(batch, seq, 3 * d_attn)
    spec = lambda off: pl.BlockSpec((None, seq, LANES), lambda b, h: (b, 0, off + h))
    blocks_per_super = ATTN_SUPER // ATTN_BLOCK
    out = pl.pallas_call(
        functools.partial(_attn_kernel, seq=seq),
        grid=(batch, pairs),
        in_specs=[spec(0), spec(pairs), spec(2 * pairs),
                  pl.BlockSpec(bias.shape, lambda b, h: (0, 0, 0))],
        out_specs=pl.BlockSpec((None, seq, LANES), lambda b, h: (b, 0, h)),
        out_shape=jax.ShapeDtypeStruct((batch, seq, d_attn), MXU_DTYPE),
        scratch_shapes=[pltpu.VMEM((blocks_per_super, 2 * ATTN_BLOCK, KEY_PAD), F32),
                        pltpu.VMEM((blocks_per_super, 2 * ATTN_BLOCK, KEY_PAD), MXU_DTYPE),
                        pltpu.VMEM((blocks_per_super, 2 * ATTN_BLOCK, LANES), F32),
                        pltpu.VMEM((len(DILATIONS), ATTN_SUPER, LANES), F32),
                        pltpu.VMEM((len(DILATIONS), ATTN_SUPER, LANES), F32),
                        pltpu.VMEM((len(DILATIONS), ATTN_SUPER, LANES), F32)],
        compiler_params=_params("parallel", "parallel"),
        name="dilated_attention",
    )(qkv3, qkv3, qkv3, bias)
    return out.reshape(batch * seq, d_attn)


def _dft_stage1_kernel(w_ref, x_ref, tc_ref, ts_ref, re_ref, nim_ref, *, q):
    w = w_ref[...]
    for c in range(0, x_ref.shape[1], DFT_CHUNK):
        r = jnp.dot(w, x_ref[:, c:c + DFT_CHUNK], preferred_element_type=F32)
        cr, sr = r[:q], r[q:]
        tc, ts = tc_ref[:, c:c + DFT_CHUNK], ts_ref[:, c:c + DFT_CHUNK]
        re_ref[:, c:c + DFT_CHUNK] = (cr * tc - sr * ts).astype(re_ref.dtype)
        nim_ref[:, c:c + DFT_CHUNK] = (cr * ts + sr * tc).astype(nim_ref.dtype)


def _dft_stage2_kernel(re_ref, nim_ref, w2_ref, chan_ref, o_ref, *, p):
    for r in range(0, re_ref.shape[0], p):
        t = jnp.concatenate([re_ref[r:r + p, :], nim_ref[r:r + p, :]], axis=0)
        z = jnp.dot(w2_ref[...], t, preferred_element_type=F32)
        zz = jnp.concatenate([z[:p], z[p:]], axis=1).astype(MXU_DTYPE)
        o_ref[r:r + p, :] = jnp.dot(zz, chan_ref[...], preferred_element_type=F32).astype(o_ref.dtype)


def _dft_tables(seq, d_four):
    p = DFT_P
    q = seq // p

    def cos_sin(n_rows, n_cols, period):
        i = jnp.arange(n_rows, dtype=jnp.int32)[:, None] * jnp.arange(n_cols, dtype=jnp.int32)[None, :]
        ang = (i % period).astype(F32) * (2.0 * math.pi / period)
        return jnp.cos(ang), jnp.sin(ang)

    cq, sq = cos_sin(q, q, q)
    w1 = (jnp.concatenate([cq, sq], axis=0) * q ** -0.5).astype(MXU_DTYPE)
    tc, ts = cos_sin(q, p, seq)
    tc, ts = jnp.repeat(tc, d_four, axis=1), jnp.repeat(ts, d_four, axis=1)
    cp, sp = cos_sin(p, p, p)
    w2 = (jnp.block([[cp, -sp], [sp, cp]]) * p ** -0.5).astype(MXU_DTYPE)
    gw = d_four // FOURIER_GROUPS
    cc, sc = cos_sin(gw, gw, gw)
    eye = jnp.eye(FOURIER_GROUPS, dtype=F32)
    chan = (jnp.concatenate([jnp.kron(eye, cc), -jnp.kron(eye, sc)], axis=0) * gw ** -0.5).astype(MXU_DTYPE)
    return w1, tc, ts, w2, chan


def _fourier(uf, dft, batch, seq):
    w1, tc, ts, w2, chan = dft
    d_four = uf.shape[1]
    p = DFT_P
    q = seq // p
    flat = p * d_four
    x3 = uf.reshape(batch, q, flat)
    xspec = pl.BlockSpec((None, q, DFT_LANES), lambda j, b: (b, 0, j))
    tspec = pl.BlockSpec((q, DFT_LANES), lambda j, b: (0, j))
    t_re, t_nim = pl.pallas_call(
        functools.partial(_dft_stage1_kernel, q=q),
        grid=(flat // DFT_LANES, batch),
        in_specs=[pl.BlockSpec((2 * q, q), lambda j, b: (0, 0)), xspec, tspec, tspec],
        out_specs=[xspec, xspec],
        out_shape=[jax.ShapeDtypeStruct((batch, q, flat), MXU_DTYPE)] * 2,
        compiler_params=_params("parallel", "arbitrary"),
        name="fourier_stage1",
    )(w1, x3, tc, ts)
    rows = DFT_GROUP * p
    rspec = pl.BlockSpec((None, rows, d_four), lambda b, i: (b, i, 0))
    y = pl.pallas_call(
        functools.partial(_dft_stage2_kernel, p=p),
        grid=(batch, q // DFT_GROUP),
        in_specs=[rspec, rspec,
                  pl.BlockSpec((2 * p, 2 * p), lambda b, i: (0, 0)),
                  pl.BlockSpec((2 * d_four, d_four), lambda b, i: (0, 0))],
        out_specs=rspec,
        out_shape=jax.ShapeDtypeStruct((batch, seq, d_four), MXU_DTYPE),
        compiler_params=_params("parallel", "parallel"),
        name="fourier_stage2",
    )(t_re.reshape(batch, seq, d_four), t_nim.reshape(batch, seq, d_four), w2, chan)
    y = y.reshape(batch, q, p, d_four).transpose(0, 2, 1, 3)
    return y.reshape(batch * seq, d_four)


def _conv_kernel(cur_ref, prev_ref, next_ref, w_ref, b_ref, g_ref, beta_ref, o_ref, hp_ref, acc_ref, *, d_conv):
    i = pl.program_id(1)
    last = pl.num_programs(1) - 1
    tile = cur_ref.shape[0]

    def glu(ref):
        return ref[:, :d_conv] * jax.nn.sigmoid(ref[:, d_conv:])

    hp_ref[pl.ds(0, CONV_HALO), :] = jnp.where(i > 0, glu(prev_ref), 0.0)
    hp_ref[pl.ds(CONV_HALO, tile), :] = glu(cur_ref)
    hp_ref[pl.ds(CONV_HALO + tile, CONV_HALO), :] = jnp.where(i < last, glu(next_ref), 0.0)

    first_tap = CONV_HALO - CONV_PAD
    for c in range(0, d_conv, LANES):
        taps = [w_ref[pl.ds(j, 1), c:c + LANES] for j in range(CONV_WIDTH)]

        def rows(rc, carry, c=c, taps=taps):
            base = pl.multiple_of(rc * CONV_ROWS, CONV_ROWS)
            win = hp_ref[pl.ds(base, CONV_ROWS + 2 * CONV_HALO), c:c + LANES]
            acc = jnp.zeros((CONV_ROWS, LANES), F32)
            for sh in range(SUBLANES):
                part = None
                for j in range(sh, CONV_WIDTH, SUBLANES):
                    term = win[j - sh:j - sh + CONV_ROWS + SUBLANES] * taps[j]
                    part = term if part is None else part + term
                acc = acc + part[first_tap + sh:first_tap + sh + CONV_ROWS]
            acc_ref[pl.ds(base, CONV_ROWS), c:c + LANES] = acc
            return carry

        lax.fori_loop(0, tile // CONV_ROWS, rows, 0)

    h = _layer_norm(acc_ref[...] + b_ref[...], g_ref[...], beta_ref[...])
    o_ref[...] = (h * jax.nn.sigmoid(h)).astype(o_ref.dtype)


def _conv_module(uc, conv_w, conv_b, ln_g, ln_b, batch, seq):
    d_conv = uc.shape[1] // 2
    tile = ROW_TILE
    halo_per_tile = tile // CONV_HALO
    n_halo = seq // CONV_HALO
    u3 = uc.reshape(batch, seq, 2 * d_conv)
    vec = pl.BlockSpec((1, d_conv), lambda b, i: (0, 0))
    out = pl.pallas_call(
        functools.partial(_conv_kernel, d_conv=d_conv),
        grid=(batch, seq // tile),
        in_specs=[pl.BlockSpec((None, tile, 2 * d_conv), lambda b, i: (b, i, 0)),
                  pl.BlockSpec((None, CONV_HALO, 2 * d_conv),
                               lambda b, i: (b, jnp.maximum(i * halo_per_tile - 1, 0), 0)),
                  pl.BlockSpec((None, CONV_HALO, 2 * d_conv),
                               lambda b, i: (b, jnp.minimum((i + 1) * halo_per_tile, n_halo - 1), 0)),
                  pl.BlockSpec((CONV_WIDTH, d_conv), lambda b, i: (0, 0)),
                  vec, vec, vec],
        out_specs=pl.BlockSpec((None, tile, d_conv), lambda b, i: (b, i, 0)),
        out_shape=jax.ShapeDtypeStruct((batch, seq, d_conv), MXU_DTYPE),
        scratch_shapes=[pltpu.VMEM((tile + 2 * CONV_HALO, d_conv), F32), pltpu.VMEM((tile, d_conv), F32)],
        compiler_params=_params("parallel", "parallel"),
        name="conv_module",
    )(u3, u3, u3, conv_w, conv_b.reshape(1, d_conv), ln_g.reshape(1, d_conv), ln_b.reshape(1, d_conv))
    return out.reshape(batch * seq, d_conv)


def _out_proj_kernel(yf_ref, ya_ref, yc_ref, x_ref, w_ref, g_ref, b_ref, wr_ref, xf_ref, xb_ref, aff_ref, *, alpha):
    d_four, d_attn = yf_ref.shape[1], ya_ref.shape[1]
    n_sub = x_ref.shape[0] // OUT_SUB_ROWS

    def project(i):
        rows = slice(i * OUT_SUB_ROWS, (i + 1) * OUT_SUB_ROWS)
        mix = jnp.dot(yf_ref[rows, :], w_ref[:d_four, :], preferred_element_type=F32)
        mix += jnp.dot(ya_ref[rows, :], w_ref[d_four:d_four + d_attn, :], preferred_element_type=F32)
        mix += jnp.dot(yc_ref[rows, :], w_ref[d_four + d_attn:, :], preferred_element_type=F32)
        return mix

    def finish(i, mix):
        rows = slice(i * OUT_SUB_ROWS, (i + 1) * OUT_SUB_ROWS)
        x1 = _layer_norm(alpha * x_ref[rows, :] + mix, g_ref[...], b_ref[...])
        xf_ref[rows, :] = x1
        xb = x1.astype(xb_ref.dtype)
        xb_ref[rows, :] = xb
        logits = jnp.dot(xb, wr_ref[...], preferred_element_type=F32)
        lt = logits.T[:N_EXPERTS, :]
        e = jnp.exp(lt - jnp.max(lt, axis=0, keepdims=True))
        aff_ref[:, rows] = e / jnp.sum(e, axis=0, keepdims=True)

    mix = project(0)
    for i in range(n_sub):
        nxt = project(i + 1) if i + 1 < n_sub else None
        finish(i, mix)
        mix = nxt


def _out_proj(yf, ya, yc, xf, w_out, layer, g, b, w_router_pad, alpha):
    n, d = xf.shape
    row = lambda w: pl.BlockSpec((ROW_TILE, w), lambda i: (i, 0))
    vec = pl.BlockSpec((1, d), lambda i: (0, 0))
    return pl.pallas_call(
        functools.partial(_out_proj_kernel, alpha=alpha),
        grid=(n // ROW_TILE,),
        in_specs=[row(yf.shape[1]), row(ya.shape[1]), row(yc.shape[1]), row(d),
                  pl.BlockSpec((None,) + w_out.shape[1:], lambda i: (layer, 0, 0), pipeline_mode=pl.Buffered(1)),
                  vec, vec,
                  pl.BlockSpec((None,) + w_router_pad.shape[1:], lambda i: (layer, 0, 0),
                               pipeline_mode=pl.Buffered(1))],
        out_specs=[row(d), row(d), pl.BlockSpec((N_EXPERTS, ROW_TILE), lambda i: (0, i))],
        out_shape=[jax.ShapeDtypeStruct((n, d), F32), jax.ShapeDtypeStruct((n, d), MXU_DTYPE),
                   jax.ShapeDtypeStruct((N_EXPERTS, n), F32)],
        compiler_params=_params("parallel"),
        name="out_proj_ln_router",
    )(yf, ya, yc, xf, w_out, g.reshape(1, d), b.reshape(1, d), w_router_pad)


def _ffn_kernel(x_ref, wg_ref, wu_ref, wd_ref, gate_ref, o_ref, acc_ref):
    f = pl.program_id(2)

    @pl.when(f == 0)
    def _():
        acc_ref[...] = jnp.zeros_like(acc_ref)

    wg, wu, wd = (w[...].astype(MXU_DTYPE) for w in (wg_ref, wu_ref, wd_ref))
    sub = min(FFN_SUB_ROWS, x_ref.shape[0])
    for r in range(0, x_ref.shape[0], sub):
        x = x_ref[r:r + sub, :]
        g = jnp.dot(x, wg, preferred_element_type=F32)
        u = jnp.dot(x, wu, preferred_element_type=F32)
        hid = (g * jax.nn.sigmoid(g) * u).astype(MXU_DTYPE)
        acc_ref[r:r + sub, :] += jnp.dot(hid, wd, preferred_element_type=F32)

    @pl.when(f == pl.num_programs(2) - 1)
    def _():
        o_ref[...] = (acc_ref[...] * gate_ref[...]).astype(o_ref.dtype)


def _expert_ffn(xe, w_gate, w_up, w_down, layer, gate, cap):
    rows, d = xe.shape
    _, n_exp, _, d_hid = w_gate.shape
    tm = min(FFN_ROW_TILE, cap)
    tf = FFN_HID_TILE
    tiles = cap // tm
    return pl.pallas_call(
        _ffn_kernel,
        grid=(n_exp, tiles, d_hid // tf),
        in_specs=[pl.BlockSpec((tm, d), lambda e, i, f: (e * tiles + i, 0)),
                  pl.BlockSpec((None, None, d, tf), lambda e, i, f: (layer, e, 0, f)),
                  pl.BlockSpec((None, None, d, tf), lambda e, i, f: (layer, e, 0, f)),
                  pl.BlockSpec((None, None, tf, d), lambda e, i, f: (layer, e, f, 0)),
                  pl.BlockSpec((tm, 1), lambda e, i, f: (e * tiles + i, 0))],
        out_specs=pl.BlockSpec((tm, d), lambda e, i, f: (e * tiles + i, 0)),
        out_shape=jax.ShapeDtypeStruct((rows, d), MXU_DTYPE),
        scratch_shapes=[pltpu.VMEM((tm, d), F32)],
        compiler_params=_params("parallel", "parallel", "arbitrary"),
        name="expert_ffn",
    )(xe, w_gate, w_up, w_down, gate)


def _combine_ln_kernel(off_ref, x_ref, g_ref, b_ref, tok_hbm, rows_hbm, xf_ref, xb_ref,
                       tok_buf, row_buf, acc_ref, sem, *, alpha):
    blk = pl.program_id(0)
    n_blk = pl.num_programs(0)
    tm, d = x_ref.shape

    def first_row(b):
        return (off_ref[b] // SEG_ALIGN) * SEG_ALIGN

    def chunks_of(b):
        return jnp.maximum((off_ref[b + 1] - first_row(b) + SEG_CHUNK - 1) // SEG_CHUNK, 1)

    def copies(b, c, slot):
        s = pl.multiple_of(first_row(b) + c * SEG_CHUNK, SEG_ALIGN)
        return (pltpu.make_async_copy(tok_hbm.at[:, pl.ds(s, SEG_CHUNK)], tok_buf.at[slot], sem.at[0, slot]),
                pltpu.make_async_copy(rows_hbm.at[pl.ds(s, SEG_CHUNK), :], row_buf.at[slot], sem.at[1, slot]))

    def start(b, c, slot):
        for cp in copies(b, c, slot):
            cp.start()

    def wait(b, c, slot):
        for cp in copies(b, c, slot):
            cp.wait()

    def start_leading(b):
        n = chunks_of(b)
        for c in range(SEG_BUFFERS):
            @pl.when(c < n)
            def _():
                start(b, c, c)

    @pl.when(blk == 0)
    def _():
        start_leading(blk)

    n_chunks = chunks_of(blk)
    acc_ref[...] = jnp.zeros_like(acc_ref)
    token = lax.broadcasted_iota(jnp.int32, (tm, SEG_CHUNK), 0) + blk * tm

    def chunk(c, carry):
        slot = c % SEG_BUFFERS
        wait(blk, c, slot)
        onehot = (tok_buf[slot] == token).astype(MXU_DTYPE)
        for dc in range(0, d, SEG_COLS):
            acc_ref[:, dc:dc + SEG_COLS] += jnp.dot(onehot, row_buf[slot, :, dc:dc + SEG_COLS],
                                                    preferred_element_type=F32)

        @pl.when(c + SEG_BUFFERS < n_chunks)
        def _():
            start(blk, c + SEG_BUFFERS, slot)

        return carry

    lax.fori_loop(0, n_chunks, chunk, 0)

    @pl.when(blk + 1 < n_blk)
    def _():
        start_leading(blk + 1)

    x2 = _layer_norm(alpha * x_ref[...] + acc_ref[...], g_ref[...], b_ref[...])
    xf_ref[...] = x2
    xb_ref[...] = x2.astype(xb_ref.dtype)


def _combine_ln(xf, offsets, tok_sorted, rows_sorted, g, b, alpha):
    n, d = xf.shape
    tm = SEG_TOKENS
    row = lambda: pl.BlockSpec((tm, d), lambda i, off: (i, 0))
    vec = lambda: pl.BlockSpec((1, d), lambda i, off: (0, 0))
    hbm = lambda: pl.BlockSpec(memory_space=pl.ANY)
    return pl.pallas_call(
        functools.partial(_combine_ln_kernel, alpha=alpha),
        grid_spec=pltpu.PrefetchScalarGridSpec(
            num_scalar_prefetch=1,
            grid=(n // tm,),
            in_specs=[row(), vec(), vec(), hbm(), hbm()],
            out_specs=[row(), row()],
            scratch_shapes=[pltpu.VMEM((SEG_BUFFERS, 1, SEG_CHUNK), jnp.int32),
                            pltpu.VMEM((SEG_BUFFERS, SEG_CHUNK, d), MXU_DTYPE),
                            pltpu.VMEM((tm, d), F32),
                            pltpu.SemaphoreType.DMA((2, SEG_BUFFERS))]),
        out_shape=[jax.ShapeDtypeStruct((n, d), F32), jax.ShapeDtypeStruct((n, d), MXU_DTYPE)],
        compiler_params=_params("arbitrary"),
        name="moe_combine_ln",
    )(offsets, xf, g.reshape(1, d), b.reshape(1, d), tok_sorted, rows_sorted)


def _moe_combine_ln(xf, xb, aff_t, w_gate, w_up, w_down, layer, g, b, alpha):
    n, d = xf.shape
    cap = max(1, CAPACITY_FACTOR * n // N_EXPERTS)
    gate, idx = lax.top_k(aff_t, cap)
    flat = idx.reshape(-1)
    xe = xb.at[flat].get(mode="promise_in_bounds")
    ye = _expert_ffn(xe, w_gate, w_up, w_down, layer, gate.reshape(-1, 1), cap)
    tok_sorted, order = lax.sort_key_val(flat, lax.iota(jnp.int32, flat.shape[0]))
    tok_sorted = jnp.pad(tok_sorted, (0, SEG_CHUNK), constant_values=-1)
    order = jnp.pad(order, (0, SEG_CHUNK))
    rows_sorted = ye.at[order].get(mode="promise_in_bounds")
    bounds = jnp.arange(n // SEG_TOKENS + 1, dtype=jnp.int32) * SEG_TOKENS
    offsets = jnp.searchsorted(tok_sorted[:flat.shape[0]], bounds, side="left").astype(jnp.int32)
    return _combine_ln(xf, offsets, tok_sorted.reshape(1, -1), rows_sorted, g, b, alpha)


def _trunk(x, weights, depth, alpha):
    batch, seq, d = x.shape
    n = batch * seq
    d_attn = (3 * d) // 8
    d_four = d // 4
    rope = _rope_tables(seq)
    dft = _dft_tables(seq, d_four)
    bias = _attn_bias()
    xf, xb = _emb_ln(x.reshape(n, d), weights["emb_ln_g"], weights["emb_ln_b"])
    big = ("w_in", "w_out", "w_router", "w_gate", "w_up", "w_down")
    for l in range(depth):
        w = {k: v[l] for k, v in weights.items() if not k.startswith("emb_") and k not in big}
        qkv, uf, uc = _in_proj(xb, weights["w_in"], l, rope, seq, d_attn, d_four)
        ya = _attention(qkv, bias, batch, seq, d_attn)
        yf = _fourier(uf, dft, batch, seq)
        yc = _conv_module(uc, w["conv_w"], w["conv_b"], w["conv_ln_g"], w["conv_ln_b"], batch, seq)
        x1f, x1b, aff_t = _out_proj(yf, ya, yc, xf, weights["w_out"], l, w["ln1_g"], w["ln1_b"],
                                    weights["w_router"], alpha)
        xf, xb = _moe_combine_ln(x1f, x1b, aff_t, weights["w_gate"], weights["w_up"], weights["w_down"], l,
                                 w["ln2_g"], w["ln2_b"], alpha)
    return xf.reshape(batch, seq, d)


def kernel(x_prompt, x_sample, emb_ln_g, emb_ln_b, w_in, conv_w, conv_b, conv_ln_g, conv_ln_b, w_out, ln1_g, ln1_b, w_router, w_gate, w_up, w_down, ln2_g, ln2_b):
    depth = w_in.shape[0]
    alpha = (2 * depth) ** 0.25
    w_router_pad = jnp.pad(w_router, ((0, 0), (0, 0), (0, LANES - w_router.shape[-1])))
    weights = dict(emb_ln_g=emb_ln_g, emb_ln_b=emb_ln_b, conv_w=conv_w, conv_b=conv_b, conv_ln_g=conv_ln_g,
                   conv_ln_b=conv_ln_b, ln1_g=ln1_g, ln1_b=ln1_b, ln2_g=ln2_g, ln2_b=ln2_b,
                   w_in=w_in.astype(MXU_DTYPE), w_out=w_out.astype(MXU_DTYPE),
                   w_router=w_router_pad.astype(MXU_DTYPE), w_gate=w_gate, w_up=w_up, w_down=w_down)
    y_prompt = _trunk(x_prompt, weights, depth, alpha)
    y_sample = _trunk(x_sample, weights, depth, alpha)
    return (y_prompt, y_sample)
```

```python
import functools
import math

import jax
import jax.numpy as jnp
from jax import lax
from jax.experimental import pallas as pl
from jax.experimental.pallas import tpu as pltpu

F32 = jnp.float32
MXU_DTYPE = jnp.bfloat16

HEAD_DIM = 64
FOURIER_GROUPS = 4
CONV_WIDTH = 31
CONV_PAD = CONV_WIDTH // 2
DILATIONS = (1, 4, 16)
ATTN_RADIUS = 64
ATTN_BLOCK = 64
ROPE_THETA = 500000.0
ROPE_DIM = HEAD_DIM // 4
N_EXPERTS = 16
CAPACITY_FACTOR = 2
LN_EPS = 1e-5
NEG_INF = -1e30
LOG2_E = math.log2(math.e)

LANES = 128
SUBLANES = 8
VMEM_LIMIT = 56 * 1024 * 1024

ROW_TILE = 512
ATTN_SUPER = ATTN_BLOCK * max(DILATIONS)
KEY_WINDOW = 3 * ATTN_BLOCK
KEY_PAD = 2 * LANES
SOFTMAX_ROWS = 32
MIX_ROWS = 64
ATTN_SKEW = 6
CONV_HALO = 16
CONV_ROWS = 64
OUT_SUB_ROWS = 256
FFN_ROW_TILE = 1024
FFN_HID_TILE = 512
FFN_SUB_ROWS = 512
SEG_TOKENS = 256
SEG_CHUNK = 256
SEG_ALIGN = 128
SEG_COLS = 512
SEG_BUFFERS = 4
DFT_P = 128
DFT_LANES = 8192
DFT_CHUNK = 1024
DFT_GROUP = 8


def _params(*sem):
    return pltpu.CompilerParams(dimension_semantics=sem, vmem_limit_bytes=VMEM_LIMIT)


def _layer_norm(x, g, b):
    mu = jnp.mean(x, axis=-1, keepdims=True)
    xc = x - mu
    var = jnp.mean(xc * xc, axis=-1, keepdims=True)
    return xc * lax.rsqrt(var + LN_EPS) * g + b


def _emb_ln_kernel(x_ref, g_ref, b_ref, xf_ref, xb_ref):
    y = _layer_norm(x_ref[...], g_ref[...], b_ref[...])
    xf_ref[...] = y
    xb_ref[...] = y.astype(xb_ref.dtype)


def _emb_ln(x, g, b):
    n, d = x.shape
    row = pl.BlockSpec((ROW_TILE, d), lambda i: (i, 0))
    vec = pl.BlockSpec((1, d), lambda i: (0, 0))
    return pl.pallas_call(
        _emb_ln_kernel,
        grid=(n // ROW_TILE,),
        in_specs=[row, vec, vec],
        out_specs=[row, row],
        out_shape=[jax.ShapeDtypeStruct((n, d), F32), jax.ShapeDtypeStruct((n, d), MXU_DTYPE)],
        compiler_params=_params("parallel"),
        name="emb_ln",
    )(x, g.reshape(1, d), b.reshape(1, d))


def _in_proj_kernel(x_ref, w_ref, cos_ref, sa_ref, sb_ref, qkv_ref, uf_ref, uc_ref, *, d_attn, d_four):
    x = x_ref[...]
    cos, sa, sb = cos_ref[...], sa_ref[...], sb_ref[...]
    half = ROPE_DIM // 2
    chunk = 2 * LANES
    d_qkv = 3 * d_attn
    for c in range(0, d_qkv, chunk):
        acc = jnp.dot(x, w_ref[:, c:c + chunk], preferred_element_type=F32)
        if c < 2 * d_attn:
            for h in range(0, chunk, LANES):
                a = acc[:, h:h + LANES]
                rot = a * cos + pltpu.roll(a, LANES - half, 1) * sa + pltpu.roll(a, half, 1) * sb
                qkv_ref[:, c + h:c + h + LANES] = rot
        else:
            qkv_ref[:, c:c + chunk] = acc
    for c in range(0, d_four, chunk):
        acc = jnp.dot(x, w_ref[:, d_qkv + c:d_qkv + c + chunk], preferred_element_type=F32)
        uf_ref[:, c:c + chunk] = acc.astype(uf_ref.dtype)
    d_conv2 = uc_ref.shape[1]
    for c in range(0, d_conv2, chunk):
        off = d_qkv + d_four + c
        uc_ref[:, c:c + chunk] = jnp.dot(x, w_ref[:, off:off + chunk], preferred_element_type=F32)


def _rope_tables(seq):
    half = ROPE_DIM // 2
    pos = jnp.arange(seq, dtype=F32)
    inv_freq = ROPE_THETA ** (-jnp.arange(0, ROPE_DIM, 2, dtype=F32) / ROPE_DIM)
    ang = pos[:, None] * inv_freq[None, :]
    cos, sin = jnp.cos(ang), jnp.sin(ang)
    rest = HEAD_DIM - ROPE_DIM
    ones = jnp.ones((seq, rest), F32)
    zeros = jnp.zeros((seq, rest), F32)
    zh = jnp.zeros((seq, half), F32)
    cos_h = jnp.concatenate([cos, cos, ones], axis=1)
    sa_h = jnp.concatenate([-sin, zh, zeros], axis=1)
    sb_h = jnp.concatenate([zh, sin, zeros], axis=1)
    reps = LANES // HEAD_DIM
    return tuple(jnp.tile(t, (1, reps)) for t in (cos_h, sa_h, sb_h))


def _in_proj(xb, w_in, layer, rope, seq, d_attn, d_four):
    n, d = xb.shape
    d_in = w_in.shape[2]
    d_conv2 = d_in - 3 * d_attn - d_four
    tiles_per_seq = seq // ROW_TILE
    row = lambda w: pl.BlockSpec((ROW_TILE, w), lambda i: (i, 0))
    tab = pl.BlockSpec((ROW_TILE, LANES), lambda i: (i % tiles_per_seq, 0))
    return pl.pallas_call(
        functools.partial(_in_proj_kernel, d_attn=d_attn, d_four=d_four),
        grid=(n // ROW_TILE,),
        in_specs=[row(d), pl.BlockSpec((None, d, d_in), lambda i: (layer, 0, 0), pipeline_mode=pl.Buffered(1)),
                  tab, tab, tab],
        out_specs=[row(3 * d_attn), row(d_four), row(d_conv2)],
        out_shape=[jax.ShapeDtypeStruct((n, 3 * d_attn), F32),
                   jax.ShapeDtypeStruct((n, d_four), MXU_DTYPE),
                   jax.ShapeDtypeStruct((n, d_conv2), F32)],
        compiler_params=_params("parallel"),
        name="in_proj",
    )(xb, w_in, *rope)


def _attn_kernel(q_ref, k_ref, v_ref, bias_ref, o_ref, s_scr, p_scr, m_scr, osc, msc, dsc, *, seq):
    lane = lax.broadcasted_iota(jnp.int32, (ATTN_BLOCK, LANES), 1)
    head_a = lane < HEAD_DIM
    scale = HEAD_DIM ** -0.5 * LOG2_E
    blocks_per_super = ATTN_SUPER // ATTN_BLOCK
    rows2 = 2 * ATTN_BLOCK

    def strided(start, size, d):
        return pl.ds(start, size) if d == 1 else pl.ds(start, size, stride=d)

    def super_block(sb, carry):
        for bi, d in enumerate(DILATIONS):
            sub_len = seq // d
            kw = min(KEY_WINDOW, sub_len)
            per_res = blocks_per_super // d

            def place(j, d=d, sub_len=sub_len, kw=kw, per_res=per_res):
                r = j % d
                bl = j // d
                i = sb * per_res + bl
                ks = jnp.clip(ATTN_BLOCK * (i - 1), 0, sub_len - kw)
                return r, bl, i, ks

            def scores(j, d=d, kw=kw, place=place):
                r, bl, i, ks = place(j)
                q = q_ref[strided(r + d * ATTN_BLOCK * i, ATTN_BLOCK, d), :] * scale
                kk = k_ref[strided(r + d * ks, kw, d), :]
                zero = jnp.zeros_like(q)
                q2 = jnp.concatenate([jnp.where(head_a, q, zero), jnp.where(head_a, zero, q)], axis=0)
                s_scr[j, :, :kw] = lax.dot_general(q2.astype(MXU_DTYPE), kk.astype(MXU_DTYPE),
                                                   (((1,), (1,)), ((), ())), preferred_element_type=F32)

            def softmax(j, kw=kw, place=place):
                r, bl, i, ks = place(j)
                case = (ATTN_BLOCK * i - ks) // ATTN_BLOCK
                for rc in range(0, rows2, SOFTMAX_ROWS):
                    s = jnp.minimum(s_scr[j, rc:rc + SOFTMAX_ROWS, :kw], bias_ref[case, rc:rc + SOFTMAX_ROWS, :kw])
                    m = jnp.max(s, axis=-1, keepdims=True)
                    p_scr[j, rc:rc + SOFTMAX_ROWS, :kw] = jnp.exp2(s - m).astype(p_scr.dtype)
                    m_scr[j, rc:rc + SOFTMAX_ROWS, :] = jnp.broadcast_to(m, (SOFTMAX_ROWS, LANES))

            def values(j, d=d, bi=bi, kw=kw, place=place):
                r, bl, i, ks = place(j)
                vv = v_ref[strided(r + d * ks, kw, d), :].astype(MXU_DTYPE)
                vext = jnp.concatenate([vv, jnp.ones_like(vv)], axis=1)
                od = jnp.dot(p_scr[j, :, :kw], vext, preferred_element_type=F32)
                dst = strided(r + d * ATTN_BLOCK * bl, ATTN_BLOCK, d)
                osc[bi, dst, :] = jnp.where(head_a, od[:ATTN_BLOCK, :LANES], od[ATTN_BLOCK:, :LANES])
                dsc[bi, dst, :] = jnp.where(head_a, od[:ATTN_BLOCK, LANES:], od[ATTN_BLOCK:, LANES:])
                msc[bi, dst, :] = jnp.where(head_a, m_scr[j, :ATTN_BLOCK, :], m_scr[j, ATTN_BLOCK:, :])

            for t in range(blocks_per_super + 2 * ATTN_SKEW):
                if t < blocks_per_super:
                    scores(t)
                if 0 <= t - ATTN_SKEW < blocks_per_super:
                    softmax(t - ATTN_SKEW)
                if 0 <= t - 2 * ATTN_SKEW < blocks_per_super:
                    values(t - 2 * ATTN_SKEW)

        def mix(rc, carry2):
            rows = pl.ds(pl.multiple_of(rc * MIX_ROWS, MIX_ROWS), MIX_ROWS)
            m0, m1, m2 = msc[0, rows, :], msc[1, rows, :], msc[2, rows, :]
            mx = jnp.maximum(jnp.maximum(m0, m1), m2)
            e0, e1, e2 = jnp.exp2(m0 - mx), jnp.exp2(m1 - mx), jnp.exp2(m2 - mx)
            num = e0 * osc[0, rows, :] + e1 * osc[1, rows, :] + e2 * osc[2, rows, :]
            den = e0 * dsc[0, rows, :] + e1 * dsc[1, rows, :] + e2 * dsc[2, rows, :]
            out_rows = pl.ds(pl.multiple_of(sb * ATTN_SUPER + rc * MIX_ROWS, MIX_ROWS), MIX_ROWS)
            o_ref[out_rows, :] = (num / den).astype(o_ref.dtype)
            return carry2

        lax.fori_loop(0, ATTN_SUPER // MIX_ROWS, mix, 0, unroll=2)
        return carry

    lax.fori_loop(0, seq // ATTN_SUPER, super_block, 0)


def _attn_bias():
    row = jnp.arange(2 * ATTN_BLOCK, dtype=jnp.int32)[:, None] % ATTN_BLOCK
    col = jnp.arange(KEY_PAD, dtype=jnp.int32)[None, :]
    off = jnp.arange(3, dtype=jnp.int32)[:, None, None] * ATTN_BLOCK
    ok = jnp.abs(row[None] - col[None] + off) <= ATTN_RADIUS
    return jnp.where(ok, jnp.inf, NEG_INF).astype(F32)


def _attention(qkv, bias, batch, seq, d_attn):
    pairs = d_attn // LANES
    qkv3 = qkv.reshape(batch, seq, 3 * d_attn)
    spec = lambda off: pl.BlockSpec((None, seq, LANES), lambda b, h: (b, 0, off + h))
    blocks_per_super = ATTN_SUPER // ATTN_BLOCK
    out = pl.pallas_call(
        functools.partial(_attn_kernel, seq=seq),
        grid=(batch, pairs),
        in_specs=[spec(0), spec(pairs), spec(2 * pairs),
                  pl.BlockSpec(bias.shape, lambda b, h: (0, 0, 0))],
        out_specs=pl.BlockSpec((None, seq, LANES), lambda b, h: (b, 0, h)),
        out_shape=jax.ShapeDtypeStruct((batch, seq, d_attn), MXU_DTYPE),
        scratch_shapes=[pltpu.VMEM((blocks_per_super, 2 * ATTN_BLOCK, KEY_PAD), F32),
                        pltpu.VMEM((blocks_per_super, 2 * ATTN_BLOCK, KEY_PAD), MXU_DTYPE),
                        pltpu.VMEM((blocks_per_super, 2 * ATTN_BLOCK, LANES), F32),
                        pltpu.VMEM((len(DILATIONS), ATTN_SUPER, LANES), F32),
                        pltpu.VMEM((len(DILATIONS), ATTN_SUPER, LANES), F32),
                        pltpu.VMEM((len(DILATIONS), ATTN_SUPER, LANES), F32)],
        compiler_params=_params("parallel", "parallel"),
        name="dilated_attention",
    )(qkv3, qkv3, qkv3, bias)
    return out.reshape(batch * seq, d_attn)


def _dft_stage1_kernel(w_ref, x_ref, tc_ref, ts_ref, re_ref, nim_ref, *, q):
    w = w_ref[...]
    for c in range(0, x_ref.shape[1], DFT_CHUNK):
        r = jnp.dot(w, x_ref[:, c:c + DFT_CHUNK], preferred_element_type=F32)
        cr, sr = r[:q], r[q:]
        tc, ts = tc_ref[:, c:c + DFT_CHUNK], ts_ref[:, c:c + DFT_CHUNK]
        re_ref[:, c:c + DFT_CHUNK] = (cr * tc - sr * ts).astype(re_ref.dtype)
        nim_ref[:, c:c + DFT_CHUNK] = (cr * ts + sr * tc).astype(nim_ref.dtype)


def _dft_stage2_kernel(re_ref, nim_ref, w2_ref, chan_ref, o_ref, *, p):
    for r in range(0, re_ref.shape[0], p):
        t = jnp.concatenate([re_ref[r:r + p, :], nim_ref[r:r + p, :]], axis=0)
        z = jnp.dot(w2_ref[...], t, preferred_element_type=F32)
        zz = jnp.concatenate([z[:p], z[p:]], axis=1).astype(MXU_DTYPE)
        o_ref[r:r + p, :] = jnp.dot(zz, chan_ref[...], preferred_element_type=F32).astype(o_ref.dtype)


def _dft_tables(seq, d_four):
    p = DFT_P
    q = seq // p

    def cos_sin(n_rows, n_cols, period):
        i = jnp.arange(n_rows, dtype=jnp.int32)[:, None] * jnp.arange(n_cols, dtype=jnp.int32)[None, :]
        ang = (i % period).astype(F32) * (2.0 * math.pi / period)
        return jnp.cos(ang), jnp.sin(ang)

    cq, sq = cos_sin(q, q, q)
    w1 = (jnp.concatenate([cq, sq], axis=0) * q ** -0.5).astype(MXU_DTYPE)
    tc, ts = cos_sin(q, p, seq)
    tc, ts = jnp.repeat(tc, d_four, axis=1), jnp.repeat(ts, d_four, axis=1)
    cp, sp = cos_sin(p, p, p)
    w2 = (jnp.block([[cp, -sp], [sp, cp]]) * p ** -0.5).astype(MXU_DTYPE)
    gw = d_four // FOURIER_GROUPS
    cc, sc = cos_sin(gw, gw, gw)
    eye = jnp.eye(FOURIER_GROUPS, dtype=F32)
    chan = (jnp.concatenate([jnp.kron(eye, cc), -jnp.kron(eye, sc)], axis=0) * gw ** -0.5).astype(MXU_DTYPE)
    return w1, tc, ts, w2, chan


def _fourier(uf, dft, batch, seq):
    w1, tc, ts, w2, chan = dft
    d_four = uf.shape[1]
    p = DFT_P
    q = seq // p
    flat = p * d_four
    x3 = uf.reshape(batch, q, flat)
    xspec = pl.BlockSpec((None, q, DFT_LANES), lambda j, b: (b, 0, j))
    tspec = pl.BlockSpec((q, DFT_LANES), lambda j, b: (0, j))
    t_re, t_nim = pl.pallas_call(
        functools.partial(_dft_stage1_kernel, q=q),
        grid=(flat // DFT_LANES, batch),
        in_specs=[pl.BlockSpec((2 * q, q), lambda j, b: (0, 0)), xspec, tspec, tspec],
        out_specs=[xspec, xspec],
        out_shape=[jax.ShapeDtypeStruct((batch, q, flat), MXU_DTYPE)] * 2,
        compiler_params=_params("parallel", "arbitrary"),
        name="fourier_stage1",
    )(w1, x3, tc, ts)
    rows = DFT_GROUP * p
    rspec = pl.BlockSpec((None, rows, d_four), lambda b, i: (b, i, 0))
    y = pl.pallas_call(
        functools.partial(_dft_stage2_kernel, p=p),
        grid=(batch, q // DFT_GROUP),
        in_specs=[rspec, rspec,
                  pl.BlockSpec((2 * p, 2 * p), lambda b, i: (0, 0)),
                  pl.BlockSpec((2 * d_four, d_four), lambda b, i: (0, 0))],
        out_specs=rspec,
        out_shape=jax.ShapeDtypeStruct((batch, seq, d_four), MXU_DTYPE),
        compiler_params=_params("parallel", "parallel"),
        name="fourier_stage2",
    )(t_re.reshape(batch, seq, d_four), t_nim.reshape(batch, seq, d_four), w2, chan)
    y = y.reshape(batch, q, p, d_four).transpose(0, 2, 1, 3)
    return y.reshape(batch * seq, d_four)


def _conv_kernel(cur_ref, prev_ref, next_ref, w_ref, b_ref, g_ref, beta_ref, o_ref, hp_ref, acc_ref, *, d_conv):
    i = pl.program_id(1)
    last = pl.num_programs(1) - 1
    tile = cur_ref.shape[0]

    def glu(ref):
        return ref[:, :d_conv] * jax.nn.sigmoid(ref[:, d_conv:])

    hp_ref[pl.ds(0, CONV_HALO), :] = jnp.where(i > 0, glu(prev_ref), 0.0)
    hp_ref[pl.ds(CONV_HALO, tile), :] = glu(cur_ref)
    hp_ref[pl.ds(CONV_HALO + tile, CONV_HALO), :] = jnp.where(i < last, glu(next_ref), 0.0)

    first_tap = CONV_HALO - CONV_PAD
    for c in range(0, d_conv, LANES):
        taps = [w_ref[pl.ds(j, 1), c:c + LANES] for j in range(CONV_WIDTH)]

        def rows(rc, carry, c=c, taps=taps):
            base = pl.multiple_of(rc * CONV_ROWS, CONV_ROWS)
            win = hp_ref[pl.ds(base, CONV_ROWS + 2 * CONV_HALO), c:c + LANES]
            acc = jnp.zeros((CONV_ROWS, LANES), F32)
            for sh in range(SUBLANES):
                part = None
                for j in range(sh, CONV_WIDTH, SUBLANES):
                    term = win[j - sh:j - sh + CONV_ROWS + SUBLANES] * taps[j]
                    part = term if part is None else part + term
                acc = acc + part[first_tap + sh:first_tap + sh + CONV_ROWS]
            acc_ref[pl.ds(base, CONV_ROWS), c:c + LANES] = acc
            return carry

        lax.fori_loop(0, tile // CONV_ROWS, rows, 0)

    h = _layer_norm(acc_ref[...] + b_ref[...], g_ref[...], beta_ref[...])
    o_ref[...] = (h * jax.nn.sigmoid(h)).astype(o_ref.dtype)


def _conv_module(uc, conv_w, conv_b, ln_g, ln_b, batch, seq):
    d_conv = uc.shape[1] // 2
    tile = ROW_TILE
    halo_per_tile = tile // CONV_HALO
    n_halo = seq // CONV_HALO
    u3 = uc.reshape(batch, seq, 2 * d_conv)
    vec = pl.BlockSpec((1, d_conv), lambda b, i: (0, 0))
    out = pl.pallas_call(
        functools.partial(_conv_kernel, d_conv=d_conv),
        grid=(batch, seq // tile),
        in_specs=[pl.BlockSpec((None, tile, 2 * d_conv), lambda b, i: (b, i, 0)),
                  pl.BlockSpec((None, CONV_HALO, 2 * d_conv),
                               lambda b, i: (b, jnp.maximum(i * halo_per_tile - 1, 0), 0)),
                  pl.BlockSpec((None, CONV_HALO, 2 * d_conv),
                               lambda b, i: (b, jnp.minimum((i + 1) * halo_per_tile, n_halo - 1), 0)),
                  pl.BlockSpec((CONV_WIDTH, d_conv), lambda b, i: (0, 0)),
                  vec, vec, vec],
        out_specs=pl.BlockSpec((None, tile, d_conv), lambda b, i: (b, i, 0)),
        out_shape=jax.ShapeDtypeStruct((batch, seq, d_conv), MXU_DTYPE),
        scratch_shapes=[pltpu.VMEM((tile + 2 * CONV_HALO, d_conv), F32), pltpu.VMEM((tile, d_conv), F32)],
        compiler_params=_params("parallel", "parallel"),
        name="conv_module",
    )(u3, u3, u3, conv_w, conv_b.reshape(1, d_conv), ln_g.reshape(1, d_conv), ln_b.reshape(1, d_conv))
    return out.reshape(batch * seq, d_conv)


def _out_proj_kernel(yf_ref, ya_ref, yc_ref, x_ref, w_ref, g_ref, b_ref, wr_ref, xf_ref, xb_ref, aff_ref, *, alpha):
    d_four, d_attn = yf_ref.shape[1], ya_ref.shape[1]
    n_sub = x_ref.shape[0] // OUT_SUB_ROWS

    def project(i):
        rows = slice(i * OUT_SUB_ROWS, (i + 1) * OUT_SUB_ROWS)
        mix = jnp.dot(yf_ref[rows, :], w_ref[:d_four, :], preferred_element_type=F32)
        mix += jnp.dot(ya_ref[rows, :], w_ref[d_four:d_four + d_attn, :], preferred_element_type=F32)
        mix += jnp.dot(yc_ref[rows, :], w_ref[d_four + d_attn:, :], preferred_element_type=F32)
        return mix

    def finish(i, mix):
        rows = slice(i * OUT_SUB_ROWS, (i + 1) * OUT_SUB_ROWS)
        x1 = _layer_norm(alpha * x_ref[rows, :] + mix, g_ref[...], b_ref[...])
        xf_ref[rows, :] = x1
        xb = x1.astype(xb_ref.dtype)
        xb_ref[rows, :] = xb
        logits = jnp.dot(xb, wr_ref[...], preferred_element_type=F32)
        lt = logits.T[:N_EXPERTS, :]
        e = jnp.exp(lt - jnp.max(lt, axis=0, keepdims=True))
        aff_ref[:, rows] = e / jnp.sum(e, axis=0, keepdims=True)

    mix = project(0)
    for i in range(n_sub):
        nxt = project(i + 1) if i + 1 < n_sub else None
        finish(i, mix)
        mix = nxt


def _out_proj(yf, ya, yc, xf, w_out, layer, g, b, w_router_pad, alpha):
    n, d = xf.shape
    row = lambda w: pl.BlockSpec((ROW_TILE, w), lambda i: (i, 0))
    vec = pl.BlockSpec((1, d), lambda i: (0, 0))
    return pl.pallas_call(
        functools.partial(_out_proj_kernel, alpha=alpha),
        grid=(n // ROW_TILE,),
        in_specs=[row(yf.shape[1]), row(ya.shape[1]), row(yc.shape[1]), row(d),
                  pl.BlockSpec((None,) + w_out.shape[1:], lambda i: (layer, 0, 0), pipeline_mode=pl.Buffered(1)),
                  vec, vec,
                  pl.BlockSpec((None,) + w_router_pad.shape[1:], lambda i: (layer, 0, 0),
                               pipeline_mode=pl.Buffered(1))],
        out_specs=[row(d), row(d), pl.BlockSpec((N_EXPERTS, ROW_TILE), lambda i: (0, i))],
        out_shape=[jax.ShapeDtypeStruct((n, d), F32), jax.ShapeDtypeStruct((n, d), MXU_DTYPE),
                   jax.ShapeDtypeStruct((N_EXPERTS, n), F32)],
        compiler_params=_params("parallel"),
        name="out_proj_ln_router",
    )(yf, ya, yc, xf, w_out, g.reshape(1, d), b.reshape(1, d), w_router_pad)


def _ffn_kernel(x_ref, wg_ref, wu_ref, wd_ref, gate_ref, o_ref, acc_ref):
    f = pl.program_id(2)

    @pl.when(f == 0)
    def _():
        acc_ref[...] = jnp.zeros_like(acc_ref)

    wg, wu, wd = (w[...].astype(MXU_DTYPE) for w in (wg_ref, wu_ref, wd_ref))
    sub = min(FFN_SUB_ROWS, x_ref.shape[0])
    for r in range(0, x_ref.shape[0], sub):
        x = x_ref[r:r + sub, :]
        g = jnp.dot(x, wg, preferred_element_type=F32)
        u = jnp.dot(x, wu, preferred_element_type=F32)
        hid = (g * jax.nn.sigmoid(g) * u).astype(MXU_DTYPE)
        acc_ref[r:r + sub, :] += jnp.dot(hid, wd, preferred_element_type=F32)

    @pl.when(f == pl.num_programs(2) - 1)
    def _():
        o_ref[...] = (acc_ref[...] * gate_ref[...]).astype(o_ref.dtype)


def _expert_ffn(xe, w_gate, w_up, w_down, layer, gate, cap):
    rows, d = xe.shape
    _, n_exp, _, d_hid = w_gate.shape
    tm = min(FFN_ROW_TILE, cap)
    tf = FFN_HID_TILE
    tiles = cap // tm
    return pl.pallas_call(
        _ffn_kernel,
        grid=(n_exp, tiles, d_hid // tf),
        in_specs=[pl.BlockSpec((tm, d), lambda e, i, f: (e * tiles + i, 0)),
                  pl.BlockSpec((None, None, d, tf), lambda e, i, f: (layer, e, 0, f)),
                  pl.BlockSpec((None, None, d, tf), lambda e, i, f: (layer, e, 0, f)),
                  pl.BlockSpec((None, None, tf, d), lambda e, i, f: (layer, e, f, 0)),
                  pl.BlockSpec((tm, 1), lambda e, i, f: (e * tiles + i, 0))],
        out_specs=pl.BlockSpec((tm, d), lambda e, i, f: (e * tiles + i, 0)),
        out_shape=jax.ShapeDtypeStruct((rows, d), MXU_DTYPE),
        scratch_shapes=[pltpu.VMEM((tm, d), F32)],
        compiler_params=_params("parallel", "parallel", "arbitrary"),
        name="expert_ffn",
    )(xe, w_gate, w_up, w_down, gate)


def _combine_ln_kernel(off_ref, x_ref, g_ref, b_ref, tok_hbm, rows_hbm, xf_ref, xb_ref,
                       tok_buf, row_buf, acc_ref, sem, *, alpha):
    blk = pl.program_id(0)
    n_blk = pl.num_programs(0)
    tm, d = x_ref.shape

    def first_row(b):
        return (off_ref[b] // SEG_ALIGN) * SEG_ALIGN

    def chunks_of(b):
        return jnp.maximum((off_ref[b + 1] - first_row(b) + SEG_CHUNK - 1) // SEG_CHUNK, 1)

    def copies(b, c, slot):
        s = pl.multiple_of(first_row(b) + c * SEG_CHUNK, SEG_ALIGN)
        return (pltpu.make_async_copy(tok_hbm.at[:, pl.ds(s, SEG_CHUNK)], tok_buf.at[slot], sem.at[0, slot]),
                pltpu.make_async_copy(rows_hbm.at[pl.ds(s, SEG_CHUNK), :], row_buf.at[slot], sem.at[1, slot]))

    def start(b, c, slot):
        for cp in copies(b, c, slot):
            cp.start()

    def wait(b, c, slot):
        for cp in copies(b, c, slot):
            cp.wait()

    def start_leading(b):
        n = chunks_of(b)
        for c in range(SEG_BUFFERS):
            @pl.when(c < n)
            def _():
                start(b, c, c)

    @pl.when(blk == 0)
    def _():
        start_leading(blk)

    n_chunks = chunks_of(blk)
    acc_ref[...] = jnp.zeros_like(acc_ref)
    token = lax.broadcasted_iota(jnp.int32, (tm, SEG_CHUNK), 0) + blk * tm

    def chunk(c, carry):
        slot = c % SEG_BUFFERS
        wait(blk, c, slot)
        onehot = (tok_buf[slot] == token).astype(MXU_DTYPE)
        for dc in range(0, d, SEG_COLS):
            acc_ref[:, dc:dc + SEG_COLS] += jnp.dot(onehot, row_buf[slot, :, dc:dc + SEG_COLS],
                                                    preferred_element_type=F32)

        @pl.when(c + SEG_BUFFERS < n_chunks)
        def _():
            start(blk, c + SEG_BUFFERS, slot)

        return carry

    lax.fori_loop(0, n_chunks, chunk, 0)

    @pl.when(blk + 1 < n_blk)
    def _():
        start_leading(blk + 1)

    x2 = _layer_norm(alpha * x_ref[...] + acc_ref[...], g_ref[...], b_ref[...])
    xf_ref[...] = x2
    xb_ref[...] = x2.astype(xb_ref.dtype)


def _combine_ln(xf, offsets, tok_sorted, rows_sorted, g, b, alpha):
    n, d = xf.shape
    tm = SEG_TOKENS
    row = lambda: pl.BlockSpec((tm, d), lambda i, off: (i, 0))
    vec = lambda: pl.BlockSpec((1, d), lambda i, off: (0, 0))
    hbm = lambda: pl.BlockSpec(memory_space=pl.ANY)
    return pl.pallas_call(
        functools.partial(_combine_ln_kernel, alpha=alpha),
        grid_spec=pltpu.PrefetchScalarGridSpec(
            num_scalar_prefetch=1,
            grid=(n // tm,),
            in_specs=[row(), vec(), vec(), hbm(), hbm()],
            out_specs=[row(), row()],
            scratch_shapes=[pltpu.VMEM((SEG_BUFFERS, 1, SEG_CHUNK), jnp.int32),
                            pltpu.VMEM((SEG_BUFFERS, SEG_CHUNK, d), MXU_DTYPE),
                            pltpu.VMEM((tm, d), F32),
                            pltpu.SemaphoreType.DMA((2, SEG_BUFFERS))]),
        out_shape=[jax.ShapeDtypeStruct((n, d), F32), jax.ShapeDtypeStruct((n, d), MXU_DTYPE)],
        compiler_params=_params("arbitrary"),
        name="moe_combine_ln",
    )(offsets, xf, g.reshape(1, d), b.reshape(1, d), tok_sorted, rows_sorted)


def _moe_combine_ln(xf, xb, aff_t, w_gate, w_up, w_down, layer, g, b, alpha):
    n, d = xf.shape
    cap = max(1, CAPACITY_FACTOR * n // N_EXPERTS)
    gate, idx = lax.top_k(aff_t, cap)
    flat = idx.reshape(-1)
    xe = xb.at[flat].get(mode="promise_in_bounds")
    ye = _expert_ffn(xe, w_gate, w_up, w_down, layer, gate.reshape(-1, 1), cap)
    tok_sorted, order = lax.sort_key_val(flat, lax.iota(jnp.int32, flat.shape[0]))
    tok_sorted = jnp.pad(tok_sorted, (0, SEG_CHUNK), constant_values=-1)
    order = jnp.pad(order, (0, SEG_CHUNK))
    rows_sorted = ye.at[order].get(mode="promise_in_bounds")
    bounds = jnp.arange(n // SEG_TOKENS + 1, dtype=jnp.int32) * SEG_TOKENS
    before = (tok_sorted[None, :flat.shape[0]] < bounds[:, None]).astype(jnp.int32)
    offsets = jnp.sum(before, axis=1)
    return _combine_ln(xf, offsets, tok_sorted.reshape(1, -1), rows_sorted, g, b, alpha)


def _trunk(x, weights, depth, alpha):
    batch, seq, d = x.shape
    n = batch * seq
    d_attn = (3 * d) // 8
    d_four = d // 4
    rope = _rope_tables(seq)
    dft = _dft_tables(seq, d_four)
    bias = _attn_bias()
    xf, xb = _emb_ln(x.reshape(n, d), weights["emb_ln_g"], weights["emb_ln_b"])
    big = ("w_in", "w_out", "w_router", "w_gate", "w_up", "w_down")
    for l in range(depth):
        w = {k: v[l] for k, v in weights.items() if not k.startswith("emb_") and k not in big}
        qkv, uf, uc = _in_proj(xb, weights["w_in"], l, rope, seq, d_attn, d_four)
        ya = _attention(qkv, bias, batch, seq, d_attn)
        yf = _fourier(uf, dft, batch, seq)
        yc = _conv_module(uc, w["conv_w"], w["conv_b"], w["conv_ln_g"], w["conv_ln_b"], batch, seq)
        x1f, x1b, aff_t = _out_proj(yf, ya, yc, xf, weights["w_out"], l, w["ln1_g"], w["ln1_b"],
                                    weights["w_router"], alpha)
        xf, xb = _moe_combine_ln(x1f, x1b, aff_t, weights["w_gate"], weights["w_up"], weights["w_down"], l,
                                 w["ln2_g"], w["ln2_b"], alpha)
    return xf.reshape(batch, seq, d)


def kernel(x_prompt, x_sample, emb_ln_g, emb_ln_b, w_in, conv_w, conv_b, conv_ln_g, conv_ln_b, w_out, ln1_g, ln1_b, w_router, w_gate, w_up, w_down, ln2_g, ln2_b):
    depth = w_in.shape[0]
    alpha = (2 * depth) ** 0.25
    w_router_pad = jnp.pad(w_router, ((0, 0), (0, 0), (0, LANES - w_router.shape[-1])))
    weights = dict(emb_ln_g=emb_ln_g, emb_ln_b=emb_ln_b, conv_w=conv_w, conv_b=conv_b, conv_ln_g=conv_ln_g,
                   conv_ln_b=conv_ln_b, ln1_g=ln1_g, ln1_b=ln1_b, ln2_g=ln2_g, ln2_b=ln2_b,
                   w_in=w_in.astype(MXU_DTYPE), w_out=w_out.astype(MXU_DTYPE),
                   w_router=w_router_pad.astype(MXU_DTYPE), w_gate=w_gate, w_up=w_up, w_down=w_down)
    y_prompt = _trunk(x_prompt, weights, depth, alpha)
    y_sample = _trunk(x_sample, weights, depth, alpha)
    return (y_prompt, y_sample)
```
